```python
import math
import jax, jax.numpy as jnp
from jax import lax
import numpy as np

D_MODEL = 1024
BATCH = 4
SEQ = 4096
DEPTH = 4
DEC_BATCH = 128
DEC_SEQ = 8
PAST_LEN = 8192
PAGE_SIZE = 128

F32 = jnp.float32
N_EVEN = (DEPTH + 1) // 2
N_ODD = DEPTH // 2
GLA_HEADS = 4
GLA_DK = 64
GLA_DV = 128
GLA_GATE_RANK = 16
GLA_TAU = 16.0
GLA_CHUNK = 16
DSA_HEADS = 8
DSA_KV_HEADS = 2
DSA_GROUP = DSA_HEADS // DSA_KV_HEADS
HEAD_DIM = 64
IDX_HEADS = 8
IDX_DIM = 64
DSA_TOPK = 256
DSA_QBLOCK = 128
SWA_HEADS = 16
SWA_KV_HEADS = 4
SWA_GROUP = SWA_HEADS // SWA_KV_HEADS
WINDOW = 128
ROPE_THETA = 150000.0
D_FF = -(-8 * D_MODEL // (3 * 256)) * 256
ALPHA = (2.0 * DEPTH) ** 0.25
BETA = (8.0 * DEPTH) ** -0.25
LN_EPS = 1e-5
RMS_EPS = 1e-6

EVEN_COLS = (GLA_HEADS * GLA_DK, GLA_HEADS * GLA_DK, GLA_HEADS * GLA_DV, GLA_HEADS * GLA_DV, GLA_GATE_RANK,
             DSA_HEADS * HEAD_DIM, DSA_KV_HEADS * HEAD_DIM, DSA_KV_HEADS * HEAD_DIM,
             IDX_HEADS * IDX_DIM, IDX_DIM, IDX_HEADS)
EVEN_MIX = GLA_HEADS * GLA_DV + DSA_HEADS * HEAD_DIM
ODD_COLS = (SWA_HEADS * HEAD_DIM, SWA_KV_HEADS * HEAD_DIM, SWA_KV_HEADS * HEAD_DIM)
ODD_MIX = SWA_HEADS * HEAD_DIM

kernel_name = "hybrid_gla_dsa_swa_deepnorm_step"


def split_cols(h, sizes):
    return jnp.split(h, np.cumsum(sizes)[:-1].tolist(), axis=-1)


def layer_norm(x, g, b):
    xf = x.astype(F32)
    mu = xf.mean(-1, keepdims=True)
    var = jnp.square(xf - mu).mean(-1, keepdims=True)
    return ((xf - mu) * lax.rsqrt(var + LN_EPS) * g + b).astype(x.dtype)


def swiglu(x, w_gu, w_down):
    gate, up = jnp.split(x @ w_gu, 2, axis=-1)
    return (jax.nn.silu(gate) * up) @ w_down


def rope(x, pos):
    half = HEAD_DIM // 2
    inv = ROPE_THETA ** (-jnp.arange(half, dtype=F32) / half)
    ang = pos.astype(F32)[:, None] * inv[None, :]
    cos = jnp.cos(ang)[None, :, None, :]
    sin = jnp.sin(ang)[None, :, None, :]
    x1 = x[..., :half].astype(F32)
    x2 = x[..., half:].astype(F32)
    return jnp.concatenate([x1 * cos - x2 * sin, x2 * cos + x1 * sin], axis=-1).astype(x.dtype)


def gla_chunked(q, k, v, glog, s0):
    B, L, H, DK = q.shape
    DV = v.shape[-1]
    C = math.gcd(L, GLA_CHUNK)
    n = L // C
    q, k, v, glog = [t.reshape(B, n, C, H, t.shape[-1]) for t in (q, k, v, glog)]
    b = jnp.cumsum(glog.astype(F32), axis=2)
    causal = jnp.tril(jnp.ones((C, C), bool))[None, None, :, :, None, None]
    diff = b[:, :, :, None] - b[:, :, None, :]
    decay = jnp.exp(jnp.where(causal, diff, -jnp.inf))
    a = jnp.einsum('bnthd,bnshd,bntshd->bnhts', q, k, decay)
    o_intra = jnp.einsum('bnhts,bnshv->bnthv', a, v)
    q_in = q * jnp.exp(b)
    b_last = b[:, :, -1]
    k_in = k * jnp.exp(b_last[:, :, None] - b)
    kv = jnp.einsum('bnshd,bnshv->bnhdv', k_in, v)

    def step(S, inp):
        qi, dec, kvi = inp
        o = jnp.einsum('bthd,bhdv->bthv', qi, S)
        S = jnp.exp(dec)[..., None] * S + kvi
        return S, o

    S, o_inter = lax.scan(step, s0.astype(F32),
                          (jnp.moveaxis(q_in, 1, 0), jnp.moveaxis(b_last, 1, 0), jnp.moveaxis(kv, 1, 0)))
    o = o_intra + jnp.moveaxis(o_inter, 0, 1)
    return o.reshape(B, L, H, DV), S


def gla_out(o, r, norm_g):
    B, L = o.shape[:2]
    of = o.astype(F32)
    of = of * lax.rsqrt(jnp.mean(of * of, -1, keepdims=True) + RMS_EPS) * norm_g
    return (of.reshape(B, L, -1) * jax.nn.silu(r.astype(F32))).astype(r.dtype)


def indexer_scores(iq, iw, ik):
    dots = jax.nn.relu(jnp.einsum('bqhd,bsd->bqhs', iq, ik).astype(F32))
    return jnp.einsum('bqhs,bqh->bqs', dots, iw.astype(F32))


def dsa_attend(q, k_sel, v_sel, valid):
    B, Q = q.shape[:2]
    qg = q.reshape(B, Q, DSA_KV_HEADS, DSA_GROUP, HEAD_DIM)
    s = jnp.einsum('bqkgd,bqnkd->bqkgn', qg, k_sel).astype(F32) * HEAD_DIM ** -0.5
    s = jnp.where(valid[:, :, None, None, :], s, -jnp.inf)
    p = jax.nn.softmax(s, axis=-1)
    o = jnp.einsum('bqkgn,bqnkd->bqkgd', p.astype(v_sel.dtype), v_sel)
    return o.reshape(B, Q, DSA_HEADS * HEAD_DIM)


def dsa_prompt(q, k, v, iq, ik, iw):
    B, L = q.shape[:2]
    kk = min(DSA_TOPK, L // 4)
    nb = L // DSA_QBLOCK
    blk = lambda t: jnp.moveaxis(t.reshape(B, nb, DSA_QBLOCK, *t.shape[2:]), 1, 0)
    key_pos = jnp.arange(L)

    def one_block(args):
        i, qb, iqb, iwb = args
        t = i * DSA_QBLOCK + jnp.arange(DSA_QBLOCK)
        sc = indexer_scores(iqb, iwb, ik)
        sc = jnp.where((key_pos[None, :] <= t[:, None])[None], sc, -jnp.inf)
        _, idx = lax.top_k(sc, kk)
        ks = jax.vmap(lambda a, j: a[j])(k, idx)
        vs = jax.vmap(lambda a, j: a[j])(v, idx)
        return dsa_attend(qb, ks, vs, idx <= t[None, :, None])

    o = lax.map(one_block, (jnp.arange(nb), blk(q), blk(iq), blk(iw)))
    return jnp.moveaxis(o, 0, 1).reshape(B, L, -1)


def dsa_sample(layer, q, k_new, v_new, iq, ik_new, iw, cache_k, cache_v, cache_idx, page_table):
    Bd, Q = q.shape[:2]
    L = PAST_LEN + Q
    kk = min(DSA_TOPK, L // 4)
    ik_past = cache_idx[layer, page_table].reshape(Bd, PAST_LEN, IDX_DIM)
    ik_all = jnp.concatenate([ik_past, ik_new], axis=1)
    t = PAST_LEN + jnp.arange(Q)
    sc = indexer_scores(iq, iw, ik_all)
    sc = jnp.where((jnp.arange(L)[None, :] <= t[:, None])[None], sc, -jnp.inf)
    _, idx = lax.top_k(sc, kk)
    bidx = jnp.arange(Bd)[:, None, None]
    in_past = (idx < PAST_LEN)[..., None, None]
    ip = jnp.minimum(idx, PAST_LEN - 1)
    phys = page_table[bidx, ip // PAGE_SIZE]
    off = ip % PAGE_SIZE
    inew = jnp.clip(idx - PAST_LEN, 0, Q - 1)
    ks = jnp.where(in_past, cache_k[layer, phys, off], k_new[bidx, inew])
    vs = jnp.where(in_past, cache_v[layer, phys, off], v_new[bidx, inew])
    return dsa_attend(q, ks, vs, idx <= t[None, :, None])


def even_project(x, w_in, gate_w2, gate_b):
    B, L, _ = x.shape
    gq, gk, gv, gr, gg, dq, dk, dv, iq, ik, iw = split_cols(x @ w_in, EVEN_COLS)
    glog = jax.nn.log_sigmoid((gg @ gate_w2 + gate_b).astype(F32)) / GLA_TAU
    gla = (gq.reshape(B, L, GLA_HEADS, GLA_DK) * GLA_DK ** -0.5,
           gk.reshape(B, L, GLA_HEADS, GLA_DK),
           gv.reshape(B, L, GLA_HEADS, GLA_DV),
           gr,
           glog.reshape(B, L, GLA_HEADS, GLA_DK))
    dsa = (dq.reshape(B, L, DSA_HEADS, HEAD_DIM),
           dk.reshape(B, L, DSA_KV_HEADS, HEAD_DIM),
           dv.reshape(B, L, DSA_KV_HEADS, HEAD_DIM),
           iq.reshape(B, L, IDX_HEADS, IDX_DIM),
           ik,
           iw * (IDX_HEADS ** -0.5 * IDX_DIM ** -0.5))
    return gla, dsa


def even_prompt(x, w_in, gate_w2, gate_b, norm_g, w_out):
    (gq, gk, gv, gr, glog), (dq, dk, dv, iq, ik, iw) = even_project(x, w_in, gate_w2, gate_b)
    s0 = jnp.zeros((x.shape[0], GLA_HEADS, GLA_DK, GLA_DV), F32)
    o_gla, S = gla_chunked(gq, gk, gv, glog, s0)
    y_gla = gla_out(o_gla, gr, norm_g)
    y_dsa = dsa_prompt(dq, dk, dv, iq, ik, iw)
    out = jnp.concatenate([y_gla, y_dsa.astype(y_gla.dtype)], axis=-1) @ w_out
    return out, (dk, dv, ik, S)


def even_sample(layer, x, cache_k, cache_v, cache_idx, state_gla, page_table, w_in, gate_w2, gate_b, norm_g, w_out):
    (gq, gk, gv, gr, glog), (dq, dk, dv, iq, ik, iw) = even_project(x, w_in, gate_w2, gate_b)
    o_gla, S = gla_chunked(gq, gk, gv, glog, state_gla[layer])
    y_gla = gla_out(o_gla, gr, norm_g)
    y_dsa = dsa_sample(layer, dq, dk, dv, iq, ik, iw, cache_k, cache_v, cache_idx, page_table)
    out = jnp.concatenate([y_gla, y_dsa.astype(y_gla.dtype)], axis=-1) @ w_out
    return out, (dk, dv, ik, S)


def sink_attend(q, k, v, mask, sinks):
    s = jnp.einsum('bnqkgd,bnckd->bnkgqc', q, k).astype(F32) * HEAD_DIM ** -0.5
    s = jnp.where(mask[None, :, None, None], s, -jnp.inf)
    sink = sinks.astype(F32).reshape(1, 1, SWA_KV_HEADS, SWA_GROUP, 1, 1)
    m = jnp.maximum(s.max(-1, keepdims=True), sink)
    p = jnp.exp(s - m)
    p = p / (p.sum(-1, keepdims=True) + jnp.exp(sink - m))
    return jnp.einsum('bnkgqc,bnckd->bnqkgd', p.astype(v.dtype), v)


def swa_qkv(x, w_in, pos):
    B, L, _ = x.shape
    qh, kh, vh = split_cols(x @ w_in, ODD_COLS)
    q = rope(qh.reshape(B, L, SWA_HEADS, HEAD_DIM), pos)
    k = rope(kh.reshape(B, L, SWA_KV_HEADS, HEAD_DIM), pos)
    v = vh.reshape(B, L, SWA_KV_HEADS, HEAD_DIM)
    return q, k, v


def odd_prompt(x, w_in, sinks, w_out):
    B, L, _ = x.shape
    q, k, v = swa_qkv(x, w_in, jnp.arange(L))
    nb = L // WINDOW
    qb = q.reshape(B, nb, WINDOW, SWA_KV_HEADS, SWA_GROUP, HEAD_DIM)
    kb = k.reshape(B, nb, WINDOW, SWA_KV_HEADS, HEAD_DIM)
    vb = v.reshape(B, nb, WINDOW, SWA_KV_HEADS, HEAD_DIM)
    band = lambda t: jnp.concatenate([jnp.concatenate([jnp.zeros_like(t[:, :1]), t[:, :-1]], axis=1), t], axis=2)
    a = jnp.arange(WINDOW)[None, :, None]
    c = jnp.arange(2 * WINDOW)[None, None, :]
    blk = jnp.arange(nb)[:, None, None]
    mask = (c > a) & (c <= a + WINDOW) & ((blk > 0) | (c >= WINDOW))
    o = sink_attend(qb, band(kb), band(vb), mask, sinks).reshape(B, L, ODD_MIX)
    return o @ w_out, (k[:, -WINDOW:], v[:, -WINDOW:])


def odd_sample(x, buf_k, buf_v, w_in, sinks, w_out):
    Bd, Q, _ = x.shape
    pos = PAST_LEN + jnp.arange(Q)
    q, k, v = swa_qkv(x, w_in, pos)
    kc = jnp.concatenate([buf_k, k], axis=1)
    vc = jnp.concatenate([buf_v, v], axis=1)
    kpos = PAST_LEN - WINDOW + jnp.arange(WINDOW + Q)
    d = pos[:, None] - kpos[None, :]
    mask = ((d >= 0) & (d < WINDOW))[None]
    o = sink_attend(q.reshape(Bd, 1, Q, SWA_KV_HEADS, SWA_GROUP, HEAD_DIM), kc[:, None], vc[:, None], mask, sinks)
    return o.reshape(Bd, Q, ODD_MIX) @ w_out, (kc[:, -WINDOW:], vc[:, -WINDOW:])


def trunk(x, even_fn, odd_fn, ffn_w_gu, ffn_w_down, ln_g, ln_b):
    even_states, odd_states = [], []
    for l in range(DEPTH):
        if l % 2 == 0:
            mix, st = even_fn(l // 2, x)
            even_states.append(st)
        else:
            mix, st = odd_fn(l // 2, x)
            odd_states.append(st)
        h = layer_norm(ALPHA * x + mix, ln_g[l, 0], ln_b[l, 0])
        x = layer_norm(ALPHA * h + swiglu(h, ffn_w_gu[l], ffn_w_down[l]), ln_g[l, 1], ln_b[l, 1])
    ev = [jnp.stack(z) for z in zip(*even_states)]
    od = [jnp.stack(z) for z in zip(*odd_states)]
    return x, ev, od


def setup_inputs(seed: int = 0) -> dict:
    key = jax.random.key(seed)
    ks = iter(jax.random.split(key, 32))
    nrm = lambda shape, scale: jax.random.normal(next(ks), shape, F32) * scale
    n_pages = PAST_LEN // PAGE_SIZE
    n_used = DEC_BATCH * n_pages
    n_pool = n_used + (n_used + 3) // 4
    perm = jax.random.permutation(next(ks), n_pool)
    page_table = perm[:n_used].reshape(DEC_BATCH, n_pages).astype(jnp.int32)
    return {
        "x_prompt": nrm((BATCH, SEQ, D_MODEL), 1.0),
        "x_sample": nrm((DEC_BATCH, DEC_SEQ, D_MODEL), 1.0),
        "cache_k": nrm((N_EVEN, n_pool, PAGE_SIZE, DSA_KV_HEADS, HEAD_DIM), 1.0),
        "cache_v": nrm((N_EVEN, n_pool, PAGE_SIZE, DSA_KV_HEADS, HEAD_DIM), 1.0),
        "cache_idx": nrm((N_EVEN, n_pool, PAGE_SIZE, IDX_DIM), 1.0),
        "state_gla": nrm((N_EVEN, DEC_BATCH, GLA_HEADS, GLA_DK, GLA_DV), 1.0),
        "state_swa_k": nrm((N_ODD, DEC_BATCH, WINDOW, SWA_KV_HEADS, HEAD_DIM), 1.0),
        "state_swa_v": nrm((N_ODD, DEC_BATCH, WINDOW, SWA_KV_HEADS, HEAD_DIM), 1.0),
        "page_table": page_table,
        "w_in_even": nrm((N_EVEN, D_MODEL, sum(EVEN_COLS)), D_MODEL ** -0.5),
        "gla_gate_w2": nrm((N_EVEN, GLA_GATE_RANK, GLA_HEADS * GLA_DK), GLA_GATE_RANK ** -0.5),
        "gla_gate_b": nrm((N_EVEN, GLA_HEADS * GLA_DK), 0.1),
        "gla_norm_g": 1.0 + nrm((N_EVEN, GLA_DV), 0.02),
        "w_out_even": nrm((N_EVEN, EVEN_MIX, D_MODEL), EVEN_MIX ** -0.5 * BETA),
        "w_in_odd": nrm((N_ODD, D_MODEL, sum(ODD_COLS)), D_MODEL ** -0.5),
        "swa_sinks": nrm((N_ODD, SWA_HEADS), 1.0),
        "w_out_odd": nrm((N_ODD, ODD_MIX, D_MODEL), ODD_MIX ** -0.5 * BETA),
        "ffn_w_gu": nrm((DEPTH, D_MODEL, 2 * D_FF), D_MODEL ** -0.5),
        "ffn_w_down": nrm((DEPTH, D_FF, D_MODEL), D_FF ** -0.5 * BETA),
        "ln_g": 1.0 + nrm((DEPTH, 2, D_MODEL), 0.02),
        "ln_b": nrm((DEPTH, 2, D_MODEL), 0.02),
    }


def reference(x_prompt, x_sample, cache_k, cache_v, cache_idx, state_gla, state_swa_k, state_swa_v, page_table,
              w_in_even, gla_gate_w2, gla_gate_b, gla_norm_g, w_out_even, w_in_odd, swa_sinks, w_out_odd,
              ffn_w_gu, ffn_w_down, ln_g, ln_b):
    even_p = lambda i, h: even_prompt(h, w_in_even[i], gla_gate_w2[i], gla_gate_b[i], gla_norm_g[i], w_out_even[i])
    odd_p = lambda i, h: odd_prompt(h, w_in_odd[i], swa_sinks[i], w_out_odd[i])
    even_s = lambda i, h: even_sample(i, h, cache_k, cache_v, cache_idx, state_gla, page_table, w_in_even[i],
                                      gla_gate_w2[i], gla_gate_b[i], gla_norm_g[i], w_out_even[i])
    odd_s = lambda i, h: odd_sample(h, state_swa_k[i], state_swa_v[i], w_in_odd[i], swa_sinks[i], w_out_odd[i])

    y_prompt, (k_p, v_p, idx_p, gla_p), (swk_p, swv_p) = trunk(x_prompt, even_p, odd_p, ffn_w_gu, ffn_w_down, ln_g, ln_b)
    y_sample, (k_s, v_s, idx_s, gla_s), (swk_s, swv_s) = trunk(x_sample, even_s, odd_s, ffn_w_gu, ffn_w_down, ln_g, ln_b)
    return (y_prompt, y_sample, k_p, v_p, idx_p, gla_p, swk_p, swv_p, k_s, v_s, idx_s, gla_s, swk_s, swv_s)
```

```python
import functools

import jax
import jax.numpy as jnp
import numpy as np
from jax import lax
from jax.experimental import pallas as pl
from jax.experimental.pallas import tpu as pltpu

F32 = jnp.float32
BF16 = jnp.bfloat16
I32 = jnp.int32

GLA_HEADS = 4
GLA_DK = 64
GLA_DV = 128
GLA_GATE_RANK = 16
GLA_TAU = 16.0
DSA_HEADS = 8
DSA_KV_HEADS = 2
DSA_GROUP = DSA_HEADS // DSA_KV_HEADS
HEAD_DIM = 64
IDX_HEADS = 8
IDX_DIM = 64
DSA_TOPK = 256
DSA_QBLOCK = 128
SWA_HEADS = 16
SWA_KV_HEADS = 4
SWA_GROUP = SWA_HEADS // SWA_KV_HEADS
WINDOW = 128
ROPE_THETA = 150000.0
PAGE_SIZE = 128
LN_EPS = 1e-5
RMS_EPS = 1e-6

EVEN_COLS = (GLA_HEADS * GLA_DK, GLA_HEADS * GLA_DK, GLA_HEADS * GLA_DV, GLA_HEADS * GLA_DV, GLA_GATE_RANK,
             DSA_HEADS * HEAD_DIM, DSA_KV_HEADS * HEAD_DIM, DSA_KV_HEADS * HEAD_DIM,
             IDX_HEADS * IDX_DIM, IDX_DIM, IDX_HEADS)
GLA_COLS = 2 * GLA_HEADS * GLA_DK + 2 * GLA_HEADS * GLA_DV
DQ_COLS = DSA_HEADS * HEAD_DIM
IQ_COLS = IDX_HEADS * IDX_DIM
DKV_COLS = DSA_KV_HEADS * HEAD_DIM
SMALL_COLS = 128
SM_IK = 0
SM_GG = IDX_DIM
SM_IW = IDX_DIM + GLA_GATE_RANK
EVEN_SPLITS = (GLA_COLS, DQ_COLS, IQ_COLS, DKV_COLS, DKV_COLS, SMALL_COLS)

LANES = 128
SUBLANES = 8
VMEM_LIMIT_BYTES = 56 * 1024 * 1024

INT_MIN = np.int32(-(2 ** 31))
KEY_NEG_INF = np.int32(np.array(-np.inf, np.float32).view(np.int32) ^ np.int32(0x7FFFFFFF))
GLA_EXP_CLAMP = 80.0


def _cparams(sem):
    return pltpu.CompilerParams(dimension_semantics=sem, vmem_limit_bytes=VMEM_LIMIT_BYTES)


def _dot(a, b):
    return jnp.dot(a, b, preferred_element_type=F32)


def _dot_nt(a, b):
    return lax.dot_general(a, b, (((1,), (1,)), ((), ())), preferred_element_type=F32)


def _dot_tn(a, b):
    return lax.dot_general(a, b, (((0,), (0,)), ((), ())), preferred_element_type=F32)


def _layer_norm(v, g, b):
    mu = jnp.mean(v, axis=-1, keepdims=True)
    c = v - mu
    var = jnp.mean(c * c, axis=-1, keepdims=True)
    return c * lax.rsqrt(var + LN_EPS) * g + b


def _silu(v):
    return v * jax.nn.sigmoid(v)


def _float_key(s):
    bits = lax.bitcast_convert_type(s, I32)
    return jnp.where(bits < 0, bits ^ np.int32(0x7FFFFFFF), bits)


def _proj_kernel(x_ref, w_ref, *o_refs, splits):
    x = x_ref[...].astype(BF16)
    off = 0
    for o_ref, n in zip(o_refs, splits):
        o_ref[...] = _dot(x, w_ref[:, off:off + n])
        off += n


def _project(x, w, splits, tm):
    T, D = x.shape
    N = w.shape[1]
    return pl.pallas_call(
        functools.partial(_proj_kernel, splits=splits),
        grid=(T // tm,),
        in_specs=[pl.BlockSpec((tm, D), lambda i: (i, 0)),
                  pl.BlockSpec((D, N), lambda i: (0, 0))],
        out_specs=[pl.BlockSpec((tm, n), lambda i: (i, 0)) for n in splits],
        out_shape=[jax.ShapeDtypeStruct((T, n), F32) for n in splits],
        compiler_params=_cparams(("parallel",)),
        name="proj_even",
    )(x, w)


def _proj_rope_kernel(x_ref, w_ref, cos_ref, sin_ref, q_ref, k_ref, v_ref, *, nq, nk):
    x = x_ref[...].astype(BF16)
    cos = cos_ref[...]
    sin = sin_ref[...]
    o_rot = nq + 2 * nk
    for j in range(nq // LANES):
        u = _dot(x, w_ref[:, j * LANES:(j + 1) * LANES])
        r = _dot(x, w_ref[:, o_rot + j * LANES:o_rot + (j + 1) * LANES])
        q_ref[:, j * LANES:(j + 1) * LANES] = u * cos + r * sin
    for j in range(nk // LANES):
        u = _dot(x, w_ref[:, nq + j * LANES:nq + (j + 1) * LANES])
        r = _dot(x, w_ref[:, o_rot + nq + j * LANES:o_rot + nq + (j + 1) * LANES])
        k_ref[:, j * LANES:(j + 1) * LANES] = u * cos + r * sin
    v_ref[...] = _dot(x, w_ref[:, nq + nk:nq + 2 * nk])


def _project_rope(x, w, cos, sin, tm):
    T, D = x.shape
    N = w.shape[1]
    nq, nk = SWA_HEADS * HEAD_DIM, SWA_KV_HEADS * HEAD_DIM
    ntab = cos.shape[0] // tm
    return pl.pallas_call(
        functools.partial(_proj_rope_kernel, nq=nq, nk=nk),
        grid=(T // tm,),
        in_specs=[pl.BlockSpec((tm, D), lambda i: (i, 0)),
                  pl.BlockSpec((D, N), lambda i: (0, 0)),
                  pl.BlockSpec((tm, LANES), lambda i: (i % ntab, 0)),
                  pl.BlockSpec((tm, LANES), lambda i: (i % ntab, 0))],
        out_specs=[pl.BlockSpec((tm, nq), lambda i: (i, 0)),
                   pl.BlockSpec((tm, nk), lambda i: (i, 0)),
                   pl.BlockSpec((tm, nk), lambda i: (i, 0))],
        out_shape=[jax.ShapeDtypeStruct((T, nq), F32),
                   jax.ShapeDtypeStruct((T, nk), F32),
                   jax.ShapeDtypeStruct((T, nk), F32)],
        compiler_params=_cparams(("parallel",)),
        name="proj_odd_rope",
    )(x, w, cos, sin)


def _outproj_ln_kernel(*refs, n_lhs, alpha):
    x_ref = refs[0]
    lhs = refs[1:1 + n_lhs]
    ws = refs[1 + n_lhs:1 + 2 * n_lhs]
    g_ref, b_ref, o_ref = refs[1 + 2 * n_lhs:]
    acc = alpha * x_ref[...]
    for a_ref, w_ref in zip(lhs, ws):
        acc = acc + _dot(a_ref[...].astype(BF16), w_ref[...])
    o_ref[...] = _layer_norm(acc, g_ref[...], b_ref[...])


def _outproj_ln(x, lhs, ws, g, b, alpha, tm):
    T, D = x.shape
    n = len(lhs)
    in_specs = [pl.BlockSpec((tm, D), lambda i: (i, 0))]
    in_specs += [pl.BlockSpec((tm, a.shape[1]), lambda i: (i, 0)) for a in lhs]
    in_specs += [pl.BlockSpec(w.shape, lambda i: (0, 0)) for w in ws]
    in_specs += [pl.BlockSpec((1, D), lambda i: (0, 0))] * 2
    return pl.pallas_call(
        functools.partial(_outproj_ln_kernel, n_lhs=n, alpha=alpha),
        grid=(T // tm,),
        in_specs=in_specs,
        out_specs=pl.BlockSpec((tm, D), lambda i: (i, 0)),
        out_shape=jax.ShapeDtypeStruct((T, D), F32),
        compiler_params=_cparams(("parallel",)),
        name="outproj_ln",
    )(x, *lhs, *ws, g, b)


def _ffn_ln_kernel(h_ref, wgu_ref, wd_ref, g_ref, b_ref, o_ref, *, dff, chunk, alpha):
    h = h_ref[...]
    hb = h.astype(BF16)
    acc = alpha * h
    for c in range(dff // chunk):
        gate = _dot(hb, wgu_ref[:, c * chunk:(c + 1) * chunk])
        up = _dot(hb, wgu_ref[:, dff + c * chunk:dff + (c + 1) * chunk])
        act = (_silu(gate) * up).astype(BF16)
        acc = acc + _dot(act, wd_ref[c * chunk:(c + 1) * chunk, :])
    o_ref[...] = _layer_norm(acc, g_ref[...], b_ref[...])


def _ffn_ln(h, wgu, wd, g, b, alpha, tm, chunk):
    T, D = h.shape
    dff = wd.shape[0]
    return pl.pallas_call(
        functools.partial(_ffn_ln_kernel, dff=dff, chunk=chunk, alpha=alpha),
        grid=(T // tm,),
        in_specs=[pl.BlockSpec((tm, D), lambda i: (i, 0)),
                  pl.BlockSpec(wgu.shape, lambda i: (0, 0)),
                  pl.BlockSpec(wd.shape, lambda i: (0, 0)),
                  pl.BlockSpec((1, D), lambda i: (0, 0)),
                  pl.BlockSpec((1, D), lambda i: (0, 0))],
        out_specs=pl.BlockSpec((tm, D), lambda i: (i, 0)),
        out_shape=jax.ShapeDtypeStruct((T, D), F32),
        compiler_params=_cparams(("parallel",)),
        name="ffn_ln",
    )(h, wgu, wd, g, b)


def _gla_kernel(gla_ref, small_ref, s0_ref, w2_ref, gb_ref, ng_ref, y_ref, sout_ref, s_scr, *, C, TB):
    H, DK, DV = GLA_HEADS, GLA_DK, GLA_DV
    j = pl.program_id(1)

    @pl.when(j == 0)
    def _():
        s_scr[...] = s0_ref[0]

    rowc = lax.broadcasted_iota(I32, (C, C), 0)
    colc = lax.broadcasted_iota(I32, (C, C), 1)
    tril = rowc >= colc
    tril_b = tril.astype(BF16)
    eye_dk = lax.broadcasted_iota(I32, (DK, DK), 0) == lax.broadcasted_iota(I32, (DK, DK), 1)
    w2 = w2_ref[...]
    gb = gb_ref[...]
    ng = ng_ref[...]
    mid = C // 2 - 1

    for c in range(TB // C):
        rows = slice(c * C, (c + 1) * C)
        gg = small_ref[rows, SM_GG:SM_GG + GLA_GATE_RANK].astype(BF16)
        z = _dot(gg, w2) + gb
        glog = -(jnp.maximum(-z, 0.0) + jnp.log1p(jnp.exp(-jnp.abs(z)))) * (1.0 / GLA_TAU)
        g1 = glog.astype(BF16)
        r1 = glog - g1.astype(F32)
        g2 = r1.astype(BF16)
        g3 = (r1 - g2.astype(F32)).astype(BF16)
        b = _dot(tril_b, g1) + _dot(tril_b, g2) + _dot(tril_b, g3)
        for h in range(H):
            qh = gla_ref[rows, h * DK:(h + 1) * DK] * (DK ** -0.5)
            kh = gla_ref[rows, H * DK + h * DK:H * DK + (h + 1) * DK]
            vh = gla_ref[rows, 2 * H * DK + h * DV:2 * H * DK + (h + 1) * DV].astype(BF16)
            rh = gla_ref[rows, 2 * H * DK + H * DV + h * DV:2 * H * DK + H * DV + (h + 1) * DV]
            bh = b[:, h * DK:(h + 1) * DK]
            bl = bh[C - 1:C, :]
            rr = bh[mid:mid + 1, :]
            qt = (qh * jnp.exp(jnp.clip(bh - rr, -GLA_EXP_CLAMP, GLA_EXP_CLAMP))).astype(BF16)
            kt = (kh * jnp.exp(jnp.clip(rr - bh, -GLA_EXP_CLAMP, GLA_EXP_CLAMP))).astype(BF16)
            a = jnp.where(tril, _dot_nt(qt, kt), 0.0)
            s_prev = s_scr[h]
            o = _dot(a.astype(BF16), vh) + _dot((qh * jnp.exp(bh)).astype(BF16), s_prev.astype(BF16))
            kin = (kh * jnp.exp(bl - bh)).astype(BF16)
            kv = _dot_tn(kin, vh)
            dcol = jnp.sum(jnp.where(eye_dk, jnp.broadcast_to(bl, (DK, DK)), 0.0), axis=1, keepdims=True)
            s_scr[h] = jnp.exp(dcol) * s_prev + kv
            ms = jnp.mean(o * o, axis=-1, keepdims=True)
            y_ref[rows, h * DV:(h + 1) * DV] = o * lax.rsqrt(ms + RMS_EPS) * ng * _silu(rh)

    @pl.when(j == pl.num_programs(1) - 1)
    def _():
        sout_ref[0] = s_scr[...]


def _gla(gla, small, s0, w2, gb, ng, *, nseq, seqlen, C, TB):
    H, DK, DV = GLA_HEADS, GLA_DK, GLA_DV
    T = gla.shape[0]
    nj = seqlen // TB
    return pl.pallas_call(
        functools.partial(_gla_kernel, C=C, TB=TB),
        grid=(nseq, nj),
        in_specs=[pl.BlockSpec((TB, GLA_COLS), lambda b, j: (b * nj + j, 0)),
                  pl.BlockSpec((TB, SMALL_COLS), lambda b, j: (b * nj + j, 0)),
                  pl.BlockSpec((1, H, DK, DV), lambda b, j: (b, 0, 0, 0)),
                  pl.BlockSpec(w2.shape, lambda b, j: (0, 0)),
                  pl.BlockSpec(gb.shape, lambda b, j: (0, 0)),
                  pl.BlockSpec(ng.shape, lambda b, j: (0, 0))],
        out_specs=[pl.BlockSpec((TB, H * DV), lambda b, j: (b * nj + j, 0)),
                   pl.BlockSpec((1, H, DK, DV), lambda b, j: (b, 0, 0, 0))],
        out_shape=[jax.ShapeDtypeStruct((T, H * DV), F32),
                   jax.ShapeDtypeStruct((nseq, H, DK, DV), F32)],
        scratch_shapes=[pltpu.VMEM((H, DK, DV), F32)],
        compiler_params=_cparams(("parallel", "arbitrary")),
        name="gla",
    )(gla, small, s0, w2, gb, ng)


def _dsa_prompt_kernel(dq_ref, iq_ref, iwt_ref, small_ref, dk_ref, dv_ref, y_ref,
                       key_scr, bias_scr, acc_scr, m_scr, l_scr, j_scr, *, L, topk):
    QB = DSA_QBLOCK
    i = pl.program_id(1)
    nkt = i + 1
    neg_inf = F32(-jnp.inf)
    row = lax.broadcasted_iota(I32, (QB, QB), 0)
    col = lax.broadcasted_iota(I32, (QB, QB), 1)

    iw_t = iwt_ref[...] * (IDX_HEADS ** -0.5 * IDX_DIM ** -0.5)
    iq_h = [iq_ref[:, h * IDX_DIM:(h + 1) * IDX_DIM].astype(BF16) for h in range(IDX_HEADS)]

    def score_tile(kt, carry):
        ks = pl.multiple_of(kt * QB, QB)
        ik_t = small_ref[pl.ds(ks, QB), SM_IK:SM_IK + IDX_DIM].astype(BF16)
        sc = jnp.zeros((QB, QB), F32)
        for h in range(IDX_HEADS):
            sc = sc + jnp.maximum(_dot_nt(ik_t, iq_h[h]), 0.0) * iw_t[h:h + 1, :]
        sc = jnp.where((kt < i) | (row <= col), sc, neg_inf)
        key_scr[pl.ds(ks, QB), :] = _float_key(sc)
        return carry

    lax.fori_loop(0, nkt, score_tile, 0)

    def count(pred):
        def body(kt, part):
            ks = pl.multiple_of(kt * QB, QB)
            m = pred(key_scr[pl.ds(ks, QB), :], kt).astype(I32)
            return part + m.reshape(QB // SUBLANES, SUBLANES, QB).sum(axis=0)
        part = lax.fori_loop(0, nkt, body, jnp.zeros((SUBLANES, QB), I32))
        return part.sum(axis=0, keepdims=True)

    def search_bit(p, thr):
        cand = thr + jnp.left_shift(I32(1), 31 - p)
        cnt = count(lambda t, kt: t >= cand)
        return jnp.where(cnt >= topk, cand, thr)

    thr = lax.fori_loop(0, 32, search_bit, jnp.full((1, QB), INT_MIN, I32))

    need = topk - count(lambda t, kt: t > thr)
    n_eq = count(lambda t, kt: t == thr)
    excess = (n_eq > need) & (thr > KEY_NEG_INF)
    j_scr[...] = jnp.full((1, QB), L, I32)
    nbits = int(L).bit_length()

    @pl.when(jnp.max(excess.astype(I32)) > 0)
    def _():
        def search_idx(p, lim):
            cand = lim + jnp.left_shift(I32(1), nbits - 1 - p)
            cnt = count(lambda t, kt: (t == thr) & (kt * QB + row < cand))
            return jnp.where(cnt < need, cand, lim)
        lim = lax.fori_loop(0, nbits, search_idx, jnp.zeros((1, QB), I32))
        j_scr[...] = jnp.where(excess, lim, L)

    last_tie = j_scr[...]

    def bias_tile(kt, carry):
        ks = pl.multiple_of(kt * QB, QB)
        t = key_scr[pl.ds(ks, QB), :]
        sel = (t > thr) | ((t == thr) & (kt * QB + row <= last_tie))
        bias_scr[pl.ds(ks, QB), :] = jnp.where(sel & (t > KEY_NEG_INF), 0.0, neg_inf)
        return carry

    lax.fori_loop(0, nkt, bias_tile, 0)

    m_scr[...] = jnp.full(m_scr.shape, -1e30, F32)
    l_scr[...] = jnp.zeros(l_scr.shape, F32)
    acc_scr[...] = jnp.zeros(acc_scr.shape, F32)
    q_h = [(dq_ref[:, h * HEAD_DIM:(h + 1) * HEAD_DIM] * (HEAD_DIM ** -0.5)).astype(BF16)
           for h in range(DSA_HEADS)]

    def attend_tile(kt, carry):
        ks = pl.multiple_of(kt * QB, QB)
        bias = bias_scr[pl.ds(ks, QB), :]
        k_t = dk_ref[pl.ds(ks, QB), :].astype(BF16)
        v_t = dv_ref[pl.ds(ks, QB), :].astype(BF16)
        for n in range(DSA_KV_HEADS):
            k_n = k_t[:, n * HEAD_DIM:(n + 1) * HEAD_DIM]
            v_n = v_t[:, n * HEAD_DIM:(n + 1) * HEAD_DIM]
            for g in range(DSA_GROUP):
                h = n * DSA_GROUP + g
                s = _dot_nt(k_n, q_h[h]) + bias
                m_old = m_scr[h:h + 1, :]
                m_new = jnp.maximum(m_old, jnp.max(s, axis=0, keepdims=True))
                alpha = jnp.exp(m_old - m_new)
                p = jnp.exp(s - m_new)
                l_scr[h:h + 1, :] = alpha * l_scr[h:h + 1, :] + jnp.sum(p, axis=0, keepdims=True)
                pv = _dot_tn(v_n, p.astype(BF16))
                hs = slice(h * HEAD_DIM, (h + 1) * HEAD_DIM)
                acc_scr[hs, :] = alpha * acc_scr[hs, :] + pv
                m_scr[h:h + 1, :] = m_new
        return carry

    lax.fori_loop(0, nkt, attend_tile, 0)

    outs = []
    for h in range(DSA_HEADS):
        inv = 1.0 / l_scr[h:h + 1, :]
        outs.append(acc_scr[h * HEAD_DIM:(h + 1) * HEAD_DIM, :] * inv)
    y_ref[...] = jnp.concatenate(outs, axis=0).T


def _dsa_prompt(dq, iq, iw_t, small, dk, dv, *, nseq, seqlen):
    QB = DSA_QBLOCK
    T = dq.shape[0]
    nb = seqlen // QB
    topk = min(DSA_TOPK, seqlen // 4)
    return pl.pallas_call(
        functools.partial(_dsa_prompt_kernel, L=seqlen, topk=topk),
        grid=(nseq, nb),
        in_specs=[pl.BlockSpec((QB, DQ_COLS), lambda b, i: (b * nb + i, 0)),
                  pl.BlockSpec((QB, IQ_COLS), lambda b, i: (b * nb + i, 0)),
                  pl.BlockSpec((IDX_HEADS, QB), lambda b, i: (0, b * nb + i)),
                  pl.BlockSpec((seqlen, SMALL_COLS), lambda b, i: (b, 0)),
                  pl.BlockSpec((seqlen, DKV_COLS), lambda b, i: (b, 0)),
                  pl.BlockSpec((seqlen, DKV_COLS), lambda b, i: (b, 0))],
        out_specs=pl.BlockSpec((QB, DQ_COLS), lambda b, i: (b * nb + i, 0)),
        out_shape=jax.ShapeDtypeStruct((T, DQ_COLS), F32),
        scratch_shapes=[pltpu.VMEM((seqlen, QB), I32),
                        pltpu.VMEM((seqlen, QB), F32),
                        pltpu.VMEM((DQ_COLS, QB), F32),
                        pltpu.VMEM((DSA_HEADS, QB), F32),
                        pltpu.VMEM((DSA_HEADS, QB), F32),
                        pltpu.VMEM((1, QB), I32)],
        compiler_params=_cparams(("parallel", "arbitrary")),
        name="dsa_prompt",
    )(dq, iq, iw_t, small, dk, dv)


def _swa_prompt_kernel(q_ref, kp_ref, kc_ref, vp_ref, vc_ref, sink_ref, y_ref):
    W = WINDOW
    i = pl.program_id(1)
    a = lax.broadcasted_iota(I32, (W, 2 * W), 0)
    c = lax.broadcasted_iota(I32, (W, 2 * W), 1)
    mask = (c > a) & (c <= a + W) & ((i > 0) | (c >= W))
    neg_inf = F32(-jnp.inf)
    for n in range(SWA_KV_HEADS):
        ns = slice(n * HEAD_DIM, (n + 1) * HEAD_DIM)
        kk = jnp.concatenate([kp_ref[:, ns], kc_ref[:, ns]], axis=0).astype(BF16)
        vv = jnp.concatenate([vp_ref[:, ns], vc_ref[:, ns]], axis=0).astype(BF16)
        for g in range(SWA_GROUP):
            h = n * SWA_GROUP + g
            hs = slice(h * HEAD_DIM, (h + 1) * HEAD_DIM)
            qh = (q_ref[:, hs] * (HEAD_DIM ** -0.5)).astype(BF16)
            s = jnp.where(mask, _dot_nt(qh, kk), neg_inf)
            sink = sink_ref[h:h + 1, 0:1]
            m = jnp.maximum(jnp.max(s, axis=-1, keepdims=True), sink)
            p = jnp.exp(s - m)
            den = jnp.sum(p, axis=-1, keepdims=True) + jnp.exp(sink - m)
            y_ref[:, hs] = _dot(p.astype(BF16), vv) / den


def _swa_prompt(q, k, v, sinks, *, nseq, seqlen):
    W = WINDOW
    T = q.shape[0]
    nb = seqlen // W
    nq, nk = SWA_HEADS * HEAD_DIM, SWA_KV_HEADS * HEAD_DIM
    cur = lambda b, i: (b * nb + i, 0)
    prev = lambda b, i: (b * nb + jnp.maximum(i - 1, 0), 0)
    return pl.pallas_call(
        _swa_prompt_kernel,
        grid=(nseq, nb),
        in_specs=[pl.BlockSpec((W, nq), cur),
                  pl.BlockSpec((W, nk), prev), pl.BlockSpec((W, nk), cur),
                  pl.BlockSpec((W, nk), prev), pl.BlockSpec((W, nk), cur),
                  pl.BlockSpec(sinks.shape, lambda b, i: (0, 0))],
        out_specs=pl.BlockSpec((W, nq), cur),
        out_shape=jax.ShapeDtypeStruct((T, nq), F32),
        compiler_params=_cparams(("parallel", "parallel")),
        name="swa_prompt",
    )(q, k, k, v, v, sinks)


def _swa_sample_kernel(qb_ref, kn_ref, vn_ref, kt_ref, vt_ref, sink_ref, o_ref, kto_ref, vto_ref, *, Q):
    W = WINDOW
    R = SWA_HEADS * Q
    KD = SWA_KV_HEADS * HEAD_DIM
    neg_inf = F32(-jnp.inf)
    lane = lax.broadcasted_iota(I32, (KD, W), 1)

    def new_cols_t(x_ref):
        pad = jnp.concatenate([x_ref[...], jnp.zeros((W - Q, KD), F32)], axis=0)
        return pltpu.roll(pad.T, W - Q, axis=1)

    kt = kt_ref[0]
    vt = vt_ref[0]
    knt = new_cols_t(kn_ref)
    vnt = new_cols_t(vn_ref)
    kto_ref[0] = jnp.where(lane >= W - Q, knt, pltpu.roll(kt, W - Q, axis=1))
    vto_ref[0] = jnp.where(lane >= W - Q, vnt, pltpu.roll(vt, W - Q, axis=1))

    qb = (qb_ref[0] * (HEAD_DIM ** -0.5)).astype(BF16)
    qi = lax.broadcasted_iota(I32, (R, W), 0) % Q
    cc = lax.broadcasted_iota(I32, (R, W), 1)
    s_buf = jnp.where(cc > qi, _dot(qb, kt.astype(BF16)), neg_inf)
    s_new = jnp.where((cc >= W - Q) & (cc - (W - Q) <= qi), _dot(qb, knt.astype(BF16)), neg_inf)
    sink = sink_ref[...]
    m = jnp.maximum(jnp.maximum(jnp.max(s_buf, axis=-1, keepdims=True),
                                jnp.max(s_new, axis=-1, keepdims=True)), sink)
    p_buf = jnp.exp(s_buf - m)
    p_new = jnp.exp(s_new - m)
    den = jnp.sum(p_buf, axis=-1, keepdims=True) + jnp.sum(p_new, axis=-1, keepdims=True) + jnp.exp(sink - m)
    o = (_dot_nt(p_buf.astype(BF16), vt.astype(BF16)) + _dot_nt(p_new.astype(BF16), vnt.astype(BF16))) / den
    rpk = SWA_GROUP * Q
    o_ref[0] = jnp.concatenate(
        [o[n * rpk:(n + 1) * rpk, n * HEAD_DIM:(n + 1) * HEAD_DIM] for n in range(SWA_KV_HEADS)], axis=0)


def _swa_sample(qblk, k_new, v_new, kt, vt, sink_col, *, nseq, Q):
    W = WINDOW
    R = SWA_HEADS * Q
    KD = SWA_KV_HEADS * HEAD_DIM
    return pl.pallas_call(
        functools.partial(_swa_sample_kernel, Q=Q),
        grid=(nseq,),
        in_specs=[pl.BlockSpec((1, R, KD), lambda b: (b, 0, 0)),
                  pl.BlockSpec((Q, KD), lambda b: (b, 0)),
                  pl.BlockSpec((Q, KD), lambda b: (b, 0)),
                  pl.BlockSpec((1, KD, W), lambda b: (b, 0, 0)),
                  pl.BlockSpec((1, KD, W), lambda b: (b, 0, 0)),
                  pl.BlockSpec((R, 1), lambda b: (0, 0))],
        out_specs=[pl.BlockSpec((1, R, HEAD_DIM), lambda b: (b, 0, 0)),
                   pl.BlockSpec((1, KD, W), lambda b: (b, 0, 0)),
                   pl.BlockSpec((1, KD, W), lambda b: (b, 0, 0))],
        out_shape=[jax.ShapeDtypeStruct((nseq, R, HEAD_DIM), F32),
                   jax.ShapeDtypeStruct((nseq, KD, W), F32),
                   jax.ShapeDtypeStruct((nseq, KD, W), F32)],
        compiler_params=_cparams(("parallel",)),
        name="swa_sample",
    )(qblk, k_new, v_new, kt, vt, sink_col)


def _dsa_sample_kernel(pt_ref, ia_ref, iwc_ref, ikn_ref, qb_ref, kn_ref, vn_ref, ci_ref, ck_ref, cv_ref, o_ref,
                       ibuf, kbuf, vbuf, sem, *, layer, n_pages, Q, topk):
    P = PAGE_SIZE
    NP = n_pages
    past = NP * P
    KD = DSA_KV_HEADS * HEAD_DIM
    R = DSA_HEADS * Q
    b = pl.program_id(0)
    nb = pl.num_programs(0)
    neg_inf = F32(-jnp.inf)

    def copies(bb, slot):
        out = []
        for p in range(NP):
            page = pt_ref[bb, p]
            cs = slice(p * P, (p + 1) * P)
            out.append(pltpu.make_async_copy(ci_ref.at[layer, page], ibuf.at[slot, :, cs], sem.at[slot, 0]))
            out.append(pltpu.make_async_copy(ck_ref.at[layer, page], kbuf.at[slot, :, cs], sem.at[slot, 1]))
            out.append(pltpu.make_async_copy(cv_ref.at[layer, page], vbuf.at[slot, :, cs], sem.at[slot, 2]))
        return out

    slot = b % 2

    @pl.when(b == 0)
    def _():
        for cp in copies(0, 0):
            cp.start()

    @pl.when(b + 1 < nb)
    def _():
        for cp in copies(b + 1, 1 - slot):
            cp.start()

    for cp in copies(b, slot):
        cp.wait()

    ia = ia_ref[0].astype(BF16)
    iwc = iwc_ref[0] * (IDX_HEADS ** -0.5 * IDX_DIM ** -0.5)

    def head_sum(d):
        d = jnp.maximum(d, 0.0) * iwc
        acc = d[0:Q]
        for h in range(1, IDX_HEADS):
            acc = acc + d[h * Q:(h + 1) * Q]
        return acc

    sc_past = head_sum(_dot(ia, ibuf[slot].astype(BF16)))
    ikn = jnp.concatenate([ikn_ref[...], jnp.zeros((P - Q, IDX_DIM), F32)], axis=0)
    sc_new = head_sum(_dot_nt(ia, ikn.astype(BF16)))
    qrow = lax.broadcasted_iota(I32, (Q, P), 0)
    lane = lax.broadcasted_iota(I32, (Q, P), 1)
    sc_new = jnp.where(lane <= qrow, sc_new, neg_inf)
    key_past = _float_key(sc_past)
    key_new = _float_key(sc_new)

    def lane_count(m_past, m_new):
        part = m_new.astype(I32)
        mp = m_past.astype(I32)
        for t in range(past // P):
            part = part + mp[:, t * P:(t + 1) * P]
        return jnp.sum(part, axis=1, keepdims=True)

    def search_bit(p, thr):
        cand = thr + jnp.left_shift(I32(1), 31 - p)
        cnt = lane_count(key_past >= cand, key_new >= cand)
        return jnp.where(cnt >= topk, cand, thr)

    thr = lax.fori_loop(0, 32, search_bit, jnp.full((Q, 1), INT_MIN, I32))

    need = topk - lane_count(key_past > thr, key_new > thr)
    n_eq = lane_count(key_past == thr, key_new == thr)
    excess = (n_eq > need) & (thr > KEY_NEG_INF)
    idx_past = lax.broadcasted_iota(I32, (Q, past), 1)
    idx_new = past + lane
    total = past + P
    nbits = int(total).bit_length()

    def search_idx(p, lim):
        cand = lim + jnp.left_shift(I32(1), nbits - 1 - p)
        cnt = lane_count((key_past == thr) & (idx_past < cand), (key_new == thr) & (idx_new < cand))
        return jnp.where(cnt < need, cand, lim)

    lim = lax.cond(jnp.max(excess.astype(I32)) > 0,
                   lambda: lax.fori_loop(0, nbits, search_idx, jnp.zeros((Q, 1), I32)),
                   lambda: jnp.zeros((Q, 1), I32))
    last_tie = jnp.where(excess, lim, total)
    sel_past = (key_past > thr) | ((key_past == thr) & (idx_past <= last_tie))
    sel_new = ((key_new > thr) | ((key_new == thr) & (idx_new <= last_tie))) & (key_new > KEY_NEG_INF)
    bias_past = jnp.where(sel_past, 0.0, neg_inf)
    bias_new = jnp.where(sel_new, 0.0, neg_inf)

    qb = (qb_ref[0] * (HEAD_DIM ** -0.5)).astype(BF16)
    kn = jnp.concatenate([kn_ref[...], jnp.zeros((P - Q, KD), F32)], axis=0).astype(BF16)
    vn = jnp.concatenate([vn_ref[...], jnp.zeros((P - Q, KD), F32)], axis=0).astype(BF16)
    s_past = _dot(qb, kbuf[slot].astype(BF16)) + jnp.concatenate([bias_past] * DSA_HEADS, axis=0)
    s_new = _dot_nt(qb, kn) + jnp.concatenate([bias_new] * DSA_HEADS, axis=0)
    m = jnp.maximum(jnp.max(s_past, axis=-1, keepdims=True), jnp.max(s_new, axis=-1, keepdims=True))
    p_past = jnp.exp(s_past - m)
    p_new = jnp.exp(s_new - m)
    den = jnp.sum(p_past, axis=-1, keepdims=True) + jnp.sum(p_new, axis=-1, keepdims=True)
    o = (_dot_nt(p_past.astype(BF16), vbuf[slot].astype(BF16)) + _dot(p_new.astype(BF16), vn)) / den
    rpk = DSA_GROUP * Q
    o_ref[0] = jnp.concatenate(
        [o[n * rpk:(n + 1) * rpk, n * HEAD_DIM:(n + 1) * HEAD_DIM] for n in range(DSA_KV_HEADS)], axis=0)


def _dsa_sample(page_table, ia, iwc, ik_new, qblk, k_new, v_new, ci_t, ck_t, cv_t, *, layer, nseq, Q):
    P = PAGE_SIZE
    n_pages = page_table.shape[1]
    past = n_pages * P
    KD = DSA_KV_HEADS * HEAD_DIM
    R = DSA_HEADS * Q
    RI = IDX_HEADS * Q
    topk = min(DSA_TOPK, (past + Q) // 4)
    grid_spec = pltpu.PrefetchScalarGridSpec(
        num_scalar_prefetch=1,
        grid=(nseq,),
        in_specs=[pl.BlockSpec((1, RI, IDX_DIM), lambda b, pt: (b, 0, 0)),
                  pl.BlockSpec((1, RI, 1), lambda b, pt: (b, 0, 0)),
                  pl.BlockSpec((Q, IDX_DIM), lambda b, pt: (b, 0)),
                  pl.BlockSpec((1, R, KD), lambda b, pt: (b, 0, 0)),
                  pl.BlockSpec((Q, KD), lambda b, pt: (b, 0)),
                  pl.BlockSpec((Q, KD), lambda b, pt: (b, 0)),
                  pl.BlockSpec(memory_space=pl.ANY),
                  pl.BlockSpec(memory_space=pl.ANY),
                  pl.BlockSpec(memory_space=pl.ANY)],
        out_specs=pl.BlockSpec((1, R, HEAD_DIM), lambda b, pt: (b, 0, 0)),
        scratch_shapes=[pltpu.VMEM((2, IDX_DIM, past), F32),
                        pltpu.VMEM((2, KD, past), F32),
                        pltpu.VMEM((2, KD, past), F32),
                        pltpu.SemaphoreType.DMA((2, 3))])
    return pl.pallas_call(
        functools.partial(_dsa_sample_kernel, layer=layer, n_pages=n_pages, Q=Q, topk=topk),
        grid_spec=grid_spec,
        out_shape=jax.ShapeDtypeStruct((nseq, R, HEAD_DIM), F32),
        compiler_params=_cparams(("arbitrary",)),
        name="dsa_sample",
    )(page_table, ia, iwc, ik_new, qblk, k_new, v_new, ci_t, ck_t, cv_t)


def _even_weight(w):
    offs = np.cumsum((0,) + EVEN_COLS)
    seg = lambda j: w[:, offs[j]:offs[j + 1]]
    gq, gk, gv, gr, gg, dq, dk, dv, iq, ik, iw = [seg(j) for j in range(len(EVEN_COLS))]
    pad = jnp.zeros((w.shape[0], SMALL_COLS - IDX_DIM - GLA_GATE_RANK - IDX_HEADS), w.dtype)
    return jnp.concatenate([gq, gk, gv, gr, dq, iq, dk, dv, ik, gg, iw, pad], axis=1).astype(BF16)


def _rot_cols(w, nheads):
    half = HEAD_DIM // 2
    w4 = w.reshape(w.shape[0], nheads, 2, half)
    return jnp.concatenate([-w4[:, :, 1], w4[:, :, 0]], axis=-1).reshape(w.shape[0], nheads * HEAD_DIM)


def _odd_weight(w):
    nq, nk = SWA_HEADS * HEAD_DIM, SWA_KV_HEADS * HEAD_DIM
    wq, wk = w[:, :nq], w[:, nq:nq + nk]
    return jnp.concatenate([w, _rot_cols(wq, SWA_HEADS), _rot_cols(wk, SWA_KV_HEADS)], axis=1).astype(BF16)


def _rope_tables(pos, reps):
    half = HEAD_DIM // 2
    inv = ROPE_THETA ** (-jnp.arange(half, dtype=F32) / half)
    ang = pos.astype(F32)[:, None] * inv[None, :]
    cos = jnp.tile(jnp.cos(ang), (reps, 2 * LANES // HEAD_DIM))
    sin = jnp.tile(jnp.sin(ang), (reps, 2 * LANES // HEAD_DIM))
    return cos, sin


def _block_diag_queries(q, nseq, Q, n_kv, group):
    q5 = q.reshape(nseq, Q, n_kv, group, HEAD_DIM).transpose(0, 2, 3, 1, 4)
    eye = jnp.eye(n_kv, dtype=q.dtype)
    blk = q5[:, :, :, :, None, :] * eye[None, :, None, None, :, None]
    return blk.reshape(nseq, n_kv * group * Q, n_kv * HEAD_DIM)


def _rows_to_tokens(o, nseq, Q, heads):
    return o.reshape(nseq, heads, Q, HEAD_DIM).transpose(0, 2, 1, 3).reshape(nseq * Q, heads * HEAD_DIM)


def kernel(x_prompt, x_sample, cache_k, cache_v, cache_idx, state_gla, state_swa_k, state_swa_v, page_table,
           w_in_even, gla_gate_w2, gla_gate_b, gla_norm_g, w_out_even, w_in_odd, swa_sinks, w_out_odd,
           ffn_w_gu, ffn_w_down, ln_g, ln_b):
    B, L, D = x_prompt.shape
    Bd, Q, _ = x_sample.shape
    depth = ffn_w_gu.shape[0]
    d_ff = ffn_w_down.shape[1]
    n_even, n_pool = cache_k.shape[0], cache_k.shape[1]
    alpha = (2.0 * depth) ** 0.25
    tm = 512 if (B * L) % 512 == 0 and (Bd * Q) % 512 == 0 else 128
    ff_chunk = 256

    w_even = [_even_weight(w_in_even[i]) for i in range(n_even)]
    w_odd = [_odd_weight(w_in_odd[i]) for i in range(w_in_odd.shape[0])]
    gla_rows = GLA_HEADS * GLA_DV
    wo_even = [(w_out_even[i, :gla_rows].astype(BF16), w_out_even[i, gla_rows:].astype(BF16)) for i in range(n_even)]
    wo_odd = [w_out_odd[i].astype(BF16) for i in range(w_out_odd.shape[0])]
    w_gu = [ffn_w_gu[l].astype(BF16) for l in range(depth)]
    w_dn = [ffn_w_down[l].astype(BF16) for l in range(depth)]
    w2 = [gla_gate_w2[i].astype(BF16) for i in range(n_even)]

    ci_t = jnp.swapaxes(cache_idx, 2, 3)
    kd = DSA_KV_HEADS * HEAD_DIM
    ck_t = cache_k.transpose(0, 1, 3, 4, 2).reshape(n_even, n_pool, kd, PAGE_SIZE)
    cv_t = cache_v.transpose(0, 1, 3, 4, 2).reshape(n_even, n_pool, kd, PAGE_SIZE)
    skd = SWA_KV_HEADS * HEAD_DIM
    n_odd = state_swa_k.shape[0]
    swk_t = state_swa_k.transpose(0, 1, 3, 4, 2).reshape(n_odd, Bd, skd, WINDOW)
    swv_t = state_swa_v.transpose(0, 1, 3, 4, 2).reshape(n_odd, Bd, skd, WINDOW)

    cos_p, sin_p = _rope_tables(jnp.arange(L), 1)
    past_len = page_table.shape[1] * PAGE_SIZE
    cos_s, sin_s = _rope_tables(past_len + jnp.arange(Q), Bd)

    def finish_layer(l, x, lhs, ws):
        h = _outproj_ln(x, lhs, ws, ln_g[l, 0][None], ln_b[l, 0][None], alpha, tm)
        return _ffn_ln(h, w_gu[l], w_dn[l], ln_g[l, 1][None], ln_b[l, 1][None], alpha, tm, ff_chunk)

    def even_common(i, x):
        return _project(x, w_even[i], EVEN_SPLITS, tm)

    x = x_prompt.reshape(B * L, D)
    ev_p, od_p = [], []
    for l in range(depth):
        i = l // 2
        if l % 2 == 0:
            gla, dq, iq, dk, dv, small = even_common(i, x)
            s0 = jnp.zeros((B, GLA_HEADS, GLA_DK, GLA_DV), F32)
            y_gla, s_fin = _gla(gla, small, s0, w2[i], gla_gate_b[i][None], gla_norm_g[i][None],
                                nseq=B, seqlen=L, C=64, TB=256)
            iw_t = small[:, SM_IW:SM_IW + IDX_HEADS].T
            y_dsa = _dsa_prompt(dq, iq, iw_t, small, dk, dv, nseq=B, seqlen=L)
            ev_p.append((dk.reshape(B, L, DSA_KV_HEADS, HEAD_DIM), dv.reshape(B, L, DSA_KV_HEADS, HEAD_DIM),
                         small[:, SM_IK:SM_IK + IDX_DIM].reshape(B, L, IDX_DIM), s_fin))
            x = finish_layer(l, x, [y_gla, y_dsa], list(wo_even[i]))
        else:
            q, k, v = _project_rope(x, w_odd[i], cos_p, sin_p, tm)
            y = _swa_prompt(q, k, v, jnp.broadcast_to(swa_sinks[i][:, None], (SWA_HEADS, LANES)), nseq=B, seqlen=L)
            k4 = k.reshape(B, L, SWA_KV_HEADS, HEAD_DIM)
            v4 = v.reshape(B, L, SWA_KV_HEADS, HEAD_DIM)
            od_p.append((k4[:, L - WINDOW:], v4[:, L - WINDOW:]))
            x = finish_layer(l, x, [y], [wo_odd[i]])
    y_prompt = x.reshape(B, L, D)

    x = x_sample.reshape(Bd * Q, D)
    ev_s, od_s = [], []
    for l in range(depth):
        i = l // 2
        if l % 2 == 0:
            gla, dq, iq, dk, dv, small = even_common(i, x)
            y_gla, s_fin = _gla(gla, small, state_gla[i], w2[i], gla_gate_b[i][None], gla_norm_g[i][None],
                                nseq=Bd, seqlen=Q, C=Q, TB=Q)
            ik_new = small[:, SM_IK:SM_IK + IDX_DIM]
            iw = small[:, SM_IW:SM_IW + IDX_HEADS]
            ia = iq.reshape(Bd, Q, IDX_HEADS, IDX_DIM).transpose(0, 2, 1, 3).reshape(Bd, IDX_HEADS * Q, IDX_DIM)
            iwc = iw.reshape(Bd, Q, IDX_HEADS).transpose(0, 2, 1).reshape(Bd, IDX_HEADS * Q, 1)
            qblk = _block_diag_queries(dq, Bd, Q, DSA_KV_HEADS, DSA_GROUP)
            o = _dsa_sample(page_table, ia, iwc, ik_new, qblk, dk, dv, ci_t, ck_t, cv_t, layer=i, nseq=Bd, Q=Q)
            y_dsa = _rows_to_tokens(o, Bd, Q, DSA_HEADS)
            ev_s.append((dk.reshape(Bd, Q, DSA_KV_HEADS, HEAD_DIM), dv.reshape(Bd, Q, DSA_KV_HEADS, HEAD_DIM),
                         ik_new.reshape(Bd, Q, IDX_DIM), s_fin))
            x = finish_layer(l, x, [y_gla, y_dsa], list(wo_even[i]))
        else:
            q, k, v = _project_rope(x, w_odd[i], cos_s, sin_s, tm)
            qblk = _block_diag_queries(q, Bd, Q, SWA_KV_HEADS, SWA_GROUP)
            sink_col = jnp.repeat(swa_sinks[i], Q)[:, None]
            o, kt_new, vt_new = _swa_sample(qblk, k, v, swk_t[i], swv_t[i], sink_col, nseq=Bd, Q=Q)
            y = _rows_to_tokens(o, Bd, Q, SWA_HEADS)
            back = lambda t: t.reshape(Bd, SWA_KV_HEADS, HEAD_DIM, WINDOW).transpose(0, 3, 1, 2)
            od_s.append((back(kt_new), back(vt_new)))
            x = finish_layer(l, x, [y], [wo_odd[i]])
    y_sample = x.reshape(Bd, Q, D)

    stack = lambda states: [jnp.stack(z) for z in zip(*states)]
    k_p, v_p, idx_p, gla_p = stack(ev_p)
    swk_p, swv_p = stack(od_p)
    k_s, v_s, idx_s, gla_s = stack(ev_s)
    swk_s, swv_s = stack(od_s)
    return (y_prompt, y_sample, k_p, v_p, idx_p, gla_p, swk_p, swv_p, k_s, v_s, idx_s, gla_s, swk_s, swv_s)
```

```python
import functools

import jax
import jax.numpy as jnp
import numpy as np
from jax import lax
from jax.experimental import pallas as pl
from jax.experimental.pallas import tpu as pltpu

F32 = jnp.float32
BF16 = jnp.bfloat16
I32 = jnp.int32

GLA_HEADS = 4
GLA_DK = 64
GLA_DV = 128
GLA_GATE_RANK = 16
GLA_TAU = 16.0
DSA_HEADS = 8
DSA_KV_HEADS = 2
DSA_GROUP = DSA_HEADS // DSA_KV_HEADS
HEAD_DIM = 64
IDX_HEADS = 8
IDX_DIM = 64
DSA_TOPK = 256
DSA_QBLOCK = 128
SWA_HEADS = 16
SWA_KV_HEADS = 4
SWA_GROUP = SWA_HEADS // SWA_KV_HEADS
WINDOW = 128
ROPE_THETA = 150000.0
PAGE_SIZE = 128
LN_EPS = 1e-5
RMS_EPS = 1e-6

EVEN_COLS = (GLA_HEADS * GLA_DK, GLA_HEADS * GLA_DK, GLA_HEADS * GLA_DV, GLA_HEADS * GLA_DV, GLA_GATE_RANK,
             DSA_HEADS * HEAD_DIM, DSA_KV_HEADS * HEAD_DIM, DSA_KV_HEADS * HEAD_DIM,
             IDX_HEADS * IDX_DIM, IDX_DIM, IDX_HEADS)
GLA_COLS = 2 * GLA_HEADS * GLA_DK + 2 * GLA_HEADS * GLA_DV
DQ_COLS = DSA_HEADS * HEAD_DIM
IQ_COLS = IDX_HEADS * IDX_DIM
DKV_COLS = DSA_KV_HEADS * HEAD_DIM
SMALL_COLS = 128
SM_IK = 0
SM_GG = IDX_DIM
SM_IW = IDX_DIM + GLA_GATE_RANK
EVEN_SPLITS = (GLA_COLS, DQ_COLS, IQ_COLS, DKV_COLS, DKV_COLS, SMALL_COLS)

LANES = 128
SUBLANES = 8
VMEM_LIMIT_BYTES = 56 * 1024 * 1024

INT_MIN = np.int32(-(2 ** 31))
KEY_NEG_INF = np.int32(np.array(-np.inf, np.float32).view(np.int32) ^ np.int32(0x7FFFFFFF))
GLA_EXP_CLAMP = 80.0


def _cparams(sem):
    return pltpu.CompilerParams(dimension_semantics=sem, vmem_limit_bytes=VMEM_LIMIT_BYTES)


def _dot(a, b):
    return jnp.dot(a, b, preferred_element_type=F32)


def _dot_nt(a, b):
    return lax.dot_general(a, b, (((1,), (1,)), ((), ())), preferred_element_type=F32)


def _dot_tn(a, b):
    return lax.dot_general(a, b, (((0,), (0,)), ((), ())), preferred_element_type=F32)


def _layer_norm(v, g, b):
    mu = jnp.mean(v, axis=-1, keepdims=True)
    c = v - mu
    var = jnp.mean(c * c, axis=-1, keepdims=True)
    return c * lax.rsqrt(var + LN_EPS) * g + b


def _silu(v):
    return v * jax.nn.sigmoid(v)


def _pos_zero(s):
    return jnp.where(s == 0.0, 0.0, s)


def _float_key(s):
    bits = lax.bitcast_convert_type(s, I32)
    return jnp.where(bits < 0, bits ^ np.int32(0x7FFFFFFF), bits)


def _proj_kernel(x_ref, w_ref, *o_refs, splits):
    x = x_ref[...].astype(BF16)
    off = 0
    for o_ref, n in zip(o_refs, splits):
        o_ref[...] = _dot(x, w_ref[:, off:off + n])
        off += n


def _project(x, w, splits, tm):
    T, D = x.shape
    N = w.shape[1]
    return pl.pallas_call(
        functools.partial(_proj_kernel, splits=splits),
        grid=(T // tm,),
        in_specs=[pl.BlockSpec((tm, D), lambda i: (i, 0)),
                  pl.BlockSpec((D, N), lambda i: (0, 0))],
        out_specs=[pl.BlockSpec((tm, n), lambda i: (i, 0)) for n in splits],
        out_shape=[jax.ShapeDtypeStruct((T, n), F32) for n in splits],
        compiler_params=_cparams(("parallel",)),
        name="proj_even",
    )(x, w)


def _proj_rope_kernel(x_ref, w_ref, cos_ref, sin_ref, q_ref, k_ref, v_ref, *, nq, nk):
    x = x_ref[...].astype(BF16)
    cos = cos_ref[...]
    sin = sin_ref[...]
    o_rot = nq + 2 * nk
    for j in range(nq // LANES):
        u = _dot(x, w_ref[:, j * LANES:(j + 1) * LANES])
        r = _dot(x, w_ref[:, o_rot + j * LANES:o_rot + (j + 1) * LANES])
        q_ref[:, j * LANES:(j + 1) * LANES] = u * cos + r * sin
    for j in range(nk // LANES):
        u = _dot(x, w_ref[:, nq + j * LANES:nq + (j + 1) * LANES])
        r = _dot(x, w_ref[:, o_rot + nq + j * LANES:o_rot + nq + (j + 1) * LANES])
        k_ref[:, j * LANES:(j + 1) * LANES] = u * cos + r * sin
    v_ref[...] = _dot(x, w_ref[:, nq + nk:nq + 2 * nk])


def _project_rope(x, w, cos, sin, tm):
    T, D = x.shape
    N = w.shape[1]
    nq, nk = SWA_HEADS * HEAD_DIM, SWA_KV_HEADS * HEAD_DIM
    ntab = cos.shape[0] // tm
    return pl.pallas_call(
        functools.partial(_proj_rope_kernel, nq=nq, nk=nk),
        grid=(T // tm,),
        in_specs=[pl.BlockSpec((tm, D), lambda i: (i, 0)),
                  pl.BlockSpec((D, N), lambda i: (0, 0)),
                  pl.BlockSpec((tm, LANES), lambda i: (i % ntab, 0)),
                  pl.BlockSpec((tm, LANES), lambda i: (i % ntab, 0))],
        out_specs=[pl.BlockSpec((tm, nq), lambda i: (i, 0)),
                   pl.BlockSpec((tm, nk), lambda i: (i, 0)),
                   pl.BlockSpec((tm, nk), lambda i: (i, 0))],
        out_shape=[jax.ShapeDtypeStruct((T, nq), F32),
                   jax.ShapeDtypeStruct((T, nk), F32),
                   jax.ShapeDtypeStruct((T, nk), F32)],
        compiler_params=_cparams(("parallel",)),
        name="proj_odd_rope",
    )(x, w, cos, sin)


def _outproj_ln_kernel(*refs, n_lhs, alpha):
    x_ref = refs[0]
    lhs = refs[1:1 + n_lhs]
    ws = refs[1 + n_lhs:1 + 2 * n_lhs]
    g_ref, b_ref, o_ref = refs[1 + 2 * n_lhs:]
    acc = alpha * x_ref[...]
    for a_ref, w_ref in zip(lhs, ws):
        acc = acc + _dot(a_ref[...].astype(BF16), w_ref[...])
    o_ref[...] = _layer_norm(acc, g_ref[...], b_ref[...])


def _outproj_ln(x, lhs, ws, g, b, alpha, tm):
    T, D = x.shape
    n = len(lhs)
    in_specs = [pl.BlockSpec((tm, D), lambda i: (i, 0))]
    in_specs += [pl.BlockSpec((tm, a.shape[1]), lambda i: (i, 0)) for a in lhs]
    in_specs += [pl.BlockSpec(w.shape, lambda i: (0, 0)) for w in ws]
    in_specs += [pl.BlockSpec((1, D), lambda i: (0, 0))] * 2
    return pl.pallas_call(
        functools.partial(_outproj_ln_kernel, n_lhs=n, alpha=alpha),
        grid=(T // tm,),
        in_specs=in_specs,
        out_specs=pl.BlockSpec((tm, D), lambda i: (i, 0)),
        out_shape=jax.ShapeDtypeStruct((T, D), F32),
        compiler_params=_cparams(("parallel",)),
        name="outproj_ln",
    )(x, *lhs, *ws, g, b)


def _ffn_ln_kernel(h_ref, wgu_ref, wd_ref, g_ref, b_ref, o_ref, *, dff, chunk, alpha):
    h = h_ref[...]
    hb = h.astype(BF16)
    acc = alpha * h
    for c in range(dff // chunk):
        gate = _dot(hb, wgu_ref[:, c * chunk:(c + 1) * chunk])
        up = _dot(hb, wgu_ref[:, dff + c * chunk:dff + (c + 1) * chunk])
        act = (_silu(gate) * up).astype(BF16)
        acc = acc + _dot(act, wd_ref[c * chunk:(c + 1) * chunk, :])
    o_ref[...] = _layer_norm(acc, g_ref[...], b_ref[...])


def _ffn_ln(h, wgu, wd, g, b, alpha, tm, chunk):
    T, D = h.shape
    dff = wd.shape[0]
    return pl.pallas_call(
        functools.partial(_ffn_ln_kernel, dff=dff, chunk=chunk, alpha=alpha),
        grid=(T // tm,),
        in_specs=[pl.BlockSpec((tm, D), lambda i: (i, 0)),
                  pl.BlockSpec(wgu.shape, lambda i: (0, 0)),
                  pl.BlockSpec(wd.shape, lambda i: (0, 0)),
                  pl.BlockSpec((1, D), lambda i: (0, 0)),
                  pl.BlockSpec((1, D), lambda i: (0, 0))],
        out_specs=pl.BlockSpec((tm, D), lambda i: (i, 0)),
        out_shape=jax.ShapeDtypeStruct((T, D), F32),
        compiler_params=_cparams(("parallel",)),
        name="ffn_ln",
    )(h, wgu, wd, g, b)


def _gla_kernel(gla_ref, small_ref, s0_ref, w2_ref, gb_ref, ng_ref, y_ref, sout_ref, s_scr, *, C, TB):
    H, DK, DV = GLA_HEADS, GLA_DK, GLA_DV
    j = pl.program_id(1)

    @pl.when(j == 0)
    def _():
        s_scr[...] = s0_ref[0]

    rowc = lax.broadcasted_iota(I32, (C, C), 0)
    colc = lax.broadcasted_iota(I32, (C, C), 1)
    tril = rowc >= colc
    tril_b = tril.astype(BF16)
    eye_dk = lax.broadcasted_iota(I32, (DK, DK), 0) == lax.broadcasted_iota(I32, (DK, DK), 1)
    w2 = w2_ref[...]
    gb = gb_ref[...]
    ng = ng_ref[...]
    mid = C // 2 - 1

    for c in range(TB // C):
        rows = slice(c * C, (c + 1) * C)
        gg = small_ref[rows, SM_GG:SM_GG + GLA_GATE_RANK].astype(BF16)
        z = _dot(gg, w2) + gb
        glog = -(jnp.maximum(-z, 0.0) + jnp.log1p(jnp.exp(-jnp.abs(z)))) * (1.0 / GLA_TAU)
        g1 = glog.astype(BF16)
        r1 = glog - g1.astype(F32)
        g2 = r1.astype(BF16)
        g3 = (r1 - g2.astype(F32)).astype(BF16)
        b = _dot(tril_b, g1) + _dot(tril_b, g2) + _dot(tril_b, g3)
        for h in range(H):
            qh = gla_ref[rows, h * DK:(h + 1) * DK] * (DK ** -0.5)
            kh = gla_ref[rows, H * DK + h * DK:H * DK + (h + 1) * DK]
            vh = gla_ref[rows, 2 * H * DK + h * DV:2 * H * DK + (h + 1) * DV].astype(BF16)
            rh = gla_ref[rows, 2 * H * DK + H * DV + h * DV:2 * H * DK + H * DV + (h + 1) * DV]
            bh = b[:, h * DK:(h + 1) * DK]
            bl = bh[C - 1:C, :]
            rr = bh[mid:mid + 1, :]
            qt = (qh * jnp.exp(jnp.clip(bh - rr, -GLA_EXP_CLAMP, GLA_EXP_CLAMP))).astype(BF16)
            kt = (kh * jnp.exp(jnp.clip(rr - bh, -GLA_EXP_CLAMP, GLA_EXP_CLAMP))).astype(BF16)
            a = jnp.where(tril, _dot_nt(qt, kt), 0.0)
            s_prev = s_scr[h]
            o = _dot(a.astype(BF16), vh) + _dot((qh * jnp.exp(bh)).astype(BF16), s_prev.astype(BF16))
            kin = (kh * jnp.exp(bl - bh)).astype(BF16)
            kv = _dot_tn(kin, vh)
            dcol = jnp.sum(jnp.where(eye_dk, jnp.broadcast_to(bl, (DK, DK)), 0.0), axis=1, keepdims=True)
            s_scr[h] = jnp.exp(dcol) * s_prev + kv
            ms = jnp.mean(o * o, axis=-1, keepdims=True)
            y_ref[rows, h * DV:(h + 1) * DV] = o * lax.rsqrt(ms + RMS_EPS) * ng * _silu(rh)

    @pl.when(j == pl.num_programs(1) - 1)
    def _():
        sout_ref[0] = s_scr[...]


def _gla(gla, small, s0, w2, gb, ng, *, nseq, seqlen, C, TB):
    H, DK, DV = GLA_HEADS, GLA_DK, GLA_DV
    T = gla.shape[0]
    nj = seqlen // TB
    return pl.pallas_call(
        functools.partial(_gla_kernel, C=C, TB=TB),
        grid=(nseq, nj),
        in_specs=[pl.BlockSpec((TB, GLA_COLS), lambda b, j: (b * nj + j, 0)),
                  pl.BlockSpec((TB, SMALL_COLS), lambda b, j: (b * nj + j, 0)),
                  pl.BlockSpec((1, H, DK, DV), lambda b, j: (b, 0, 0, 0)),
                  pl.BlockSpec(w2.shape, lambda b, j: (0, 0)),
                  pl.BlockSpec(gb.shape, lambda b, j: (0, 0)),
                  pl.BlockSpec(ng.shape, lambda b, j: (0, 0))],
        out_specs=[pl.BlockSpec((TB, H * DV), lambda b, j: (b * nj + j, 0)),
                   pl.BlockSpec((1, H, DK, DV), lambda b, j: (b, 0, 0, 0))],
        out_shape=[jax.ShapeDtypeStruct((T, H * DV), F32),
                   jax.ShapeDtypeStruct((nseq, H, DK, DV), F32)],
        scratch_shapes=[pltpu.VMEM((H, DK, DV), F32)],
        compiler_params=_cparams(("parallel", "arbitrary")),
        name="gla",
    )(gla, small, s0, w2, gb, ng)


def _dsa_prompt_kernel(dq_ref, iq_ref, iwt_ref, small_ref, dk_ref, dvt_ref, y_ref,
                       key_scr, bias_scr, acc_scr, m_scr, l_scr, j_scr, *, L, topk, KC):
    QB = DSA_QBLOCK
    TPC = KC // QB
    G = DSA_GROUP
    i = pl.program_id(1)
    nch = (i + TPC) // TPC
    neg_inf = F32(-jnp.inf)
    row = lax.broadcasted_iota(I32, (QB, QB), 0)
    rowc = lax.broadcasted_iota(I32, (KC, QB), 0)
    qidx = i * QB + lax.broadcasted_iota(I32, (KC, QB), 1)

    iw_t = iwt_ref[...] * (IDX_HEADS ** -0.5 * IDX_DIM ** -0.5)
    iq_all = jnp.concatenate([iq_ref[:, h * IDX_DIM:(h + 1) * IDX_DIM] for h in range(IDX_HEADS)],
                             axis=0).astype(BF16)

    def score_chunk(c, carry):
        ks = pl.multiple_of(c * KC, KC)
        ik_c = small_ref[pl.ds(ks, KC), SM_IK:SM_IK + IDX_DIM].astype(BF16)
        d = _dot_nt(ik_c, iq_all)
        sc = jnp.maximum(d[:, 0:QB], 0.0) * iw_t[0:1, :]
        for h in range(1, IDX_HEADS):
            sc = sc + jnp.maximum(d[:, h * QB:(h + 1) * QB], 0.0) * iw_t[h:h + 1, :]
        sc = jnp.where(ks + rowc <= qidx, _pos_zero(sc), neg_inf)
        key_scr[pl.ds(ks, KC), :] = _float_key(sc)
        return carry

    lax.fori_loop(0, nch, score_chunk, 0)

    def count(pred):
        def body(c, part):
            for t in range(TPC):
                ks = pl.multiple_of(c * KC + t * QB, QB)
                m = pred(key_scr[pl.ds(ks, QB), :], ks).astype(I32)
                part = part + m.reshape(QB // SUBLANES, SUBLANES, QB).sum(axis=0)
            return part
        part = lax.fori_loop(0, nch, body, jnp.zeros((SUBLANES, QB), I32))
        return part.sum(axis=0, keepdims=True)

    def search_bit(p, thr):
        cand = thr + jnp.left_shift(I32(1), 31 - p)
        cnt = count(lambda t, ks: t >= cand)
        return jnp.where(cnt >= topk, cand, thr)

    thr = lax.fori_loop(0, 32, search_bit, jnp.full((1, QB), INT_MIN, I32))

    need = topk - count(lambda t, ks: t > thr)
    n_eq = count(lambda t, ks: t == thr)
    excess = (n_eq > need) & (thr > KEY_NEG_INF)
    j_scr[...] = jnp.full((1, QB), L, I32)
    nbits = int(L).bit_length()

    @pl.when(jnp.max(excess.astype(I32)) > 0)
    def _():
        def search_idx(p, lim):
            cand = lim + jnp.left_shift(I32(1), nbits - 1 - p)
            cnt = count(lambda t, ks: (t == thr) & (ks + row < cand))
            return jnp.where(cnt < need, cand, lim)
        lim = lax.fori_loop(0, nbits, search_idx, jnp.zeros((1, QB), I32))
        j_scr[...] = jnp.where(excess, lim, L)

    last_tie = j_scr[...]

    def bias_chunk(c, carry):
        for t in range(TPC):
            ks = pl.multiple_of(c * KC + t * QB, QB)
            tk = key_scr[pl.ds(ks, QB), :]
            sel = (tk > thr) | ((tk == thr) & (ks + row <= last_tie))
            bias_scr[pl.ds(ks, QB), :] = jnp.where(sel & (tk > KEY_NEG_INF), 0.0, neg_inf)
        return carry

    lax.fori_loop(0, nch, bias_chunk, 0)

    NH = DSA_HEADS * QB
    m_scr[...] = jnp.full(m_scr.shape, -1e30, F32)
    l_scr[...] = jnp.zeros(l_scr.shape, F32)
    acc_scr[...] = jnp.zeros(acc_scr.shape, F32)
    zq = jnp.zeros((QB, HEAD_DIM), F32)
    q_rows = []
    for h in range(DSA_HEADS):
        qh = dq_ref[:, h * HEAD_DIM:(h + 1) * HEAD_DIM] * (HEAD_DIM ** -0.5)
        parts = [qh if n == h // G else zq for n in range(DSA_KV_HEADS)]
        q_rows.append(jnp.concatenate(parts, axis=1))
    q_blk = jnp.concatenate(q_rows, axis=0).astype(BF16)

    def attend_chunk(c, carry):
        ks = pl.multiple_of(c * KC, KC)
        bias = bias_scr[pl.ds(ks, KC), :]
        bias = jnp.concatenate([bias] * DSA_HEADS, axis=1)
        k_c = dk_ref[pl.ds(ks, KC), :].astype(BF16)
        vt_c = dvt_ref[c].astype(BF16)
        s = _dot_nt(k_c, q_blk) + bias
        m_old = m_scr[...]
        m_new = jnp.maximum(m_old, jnp.max(s, axis=0, keepdims=True))
        alpha = jnp.exp(m_old - m_new)
        p = jnp.exp(s - m_new)
        l_scr[...] = alpha * l_scr[...] + jnp.sum(p, axis=0, keepdims=True)
        pb = p.astype(BF16)
        gw = G * QB
        for n in range(DSA_KV_HEADS):
            pv = _dot(vt_c[n * HEAD_DIM:(n + 1) * HEAD_DIM, :], pb[:, n * gw:(n + 1) * gw])
            acc_scr[:, n * gw:(n + 1) * gw] = alpha[:, n * gw:(n + 1) * gw] * acc_scr[:, n * gw:(n + 1) * gw] + pv
        m_scr[...] = m_new
        return carry

    lax.fori_loop(0, nch, attend_chunk, 0)

    o_t = acc_scr[...] / l_scr[...]
    y_ref[...] = jnp.concatenate([o_t[:, h * QB:(h + 1) * QB].T for h in range(DSA_HEADS)], axis=1)


def _dsa_prompt(dq, iq, iw_t, small, dk, dv, *, nseq, seqlen):
    QB = DSA_QBLOCK
    T = dq.shape[0]
    nb = seqlen // QB
    topk = min(DSA_TOPK, seqlen // 4)
    kc = 4 * QB if seqlen % (4 * QB) == 0 else QB
    ncs = seqlen // kc
    nh = DSA_HEADS * QB
    dvt = dv.reshape(nseq * ncs, kc, DKV_COLS).transpose(0, 2, 1)
    return pl.pallas_call(
        functools.partial(_dsa_prompt_kernel, L=seqlen, topk=topk, KC=kc),
        grid=(nseq, nb),
        in_specs=[pl.BlockSpec((QB, DQ_COLS), lambda b, i: (b * nb + i, 0)),
                  pl.BlockSpec((QB, IQ_COLS), lambda b, i: (b * nb + i, 0)),
                  pl.BlockSpec((IDX_HEADS, QB), lambda b, i: (0, b * nb + i)),
                  pl.BlockSpec((seqlen, SMALL_COLS), lambda b, i: (b, 0)),
                  pl.BlockSpec((seqlen, DKV_COLS), lambda b, i: (b, 0)),
                  pl.BlockSpec((ncs, DKV_COLS, kc), lambda b, i: (b, 0, 0))],
        out_specs=pl.BlockSpec((QB, DQ_COLS), lambda b, i: (b * nb + i, 0)),
        out_shape=jax.ShapeDtypeStruct((T, DQ_COLS), F32),
        scratch_shapes=[pltpu.VMEM((seqlen, QB), I32),
                        pltpu.VMEM((seqlen, QB), F32),
                        pltpu.VMEM((HEAD_DIM, nh), F32),
                        pltpu.VMEM((1, nh), F32),
                        pltpu.VMEM((1, nh), F32),
                        pltpu.VMEM((1, QB), I32)],
        compiler_params=_cparams(("parallel", "arbitrary")),
        name="dsa_prompt",
    )(dq, iq, iw_t, small, dk, dvt)


def _swa_prompt_kernel(q_ref, kp_ref, kc_ref, vtp_ref, vtc_ref, sink_ref, y_ref):
    W = WINDOW
    G = SWA_GROUP
    i = pl.program_id(1)
    c = lax.broadcasted_iota(I32, (2 * W, W), 0)
    a = lax.broadcasted_iota(I32, (2 * W, W), 1)
    mask = (c > a) & (c <= a + W) & ((i > 0) | (c >= W))
    bias = jnp.where(mask, 0.0, F32(-jnp.inf))
    bias = jnp.concatenate([bias] * G, axis=1)
    kk = jnp.concatenate([kp_ref[...], kc_ref[...]], axis=0).astype(BF16)
    vt = jnp.concatenate([vtp_ref[...], vtc_ref[...]], axis=1).astype(BF16)
    outs = []
    for n in range(SWA_KV_HEADS):
        ns = slice(n * HEAD_DIM, (n + 1) * HEAD_DIM)
        qn = jnp.concatenate([q_ref[:, (n * G + g) * HEAD_DIM:(n * G + g + 1) * HEAD_DIM] for g in range(G)], axis=0)
        qn = (qn * (HEAD_DIM ** -0.5)).astype(BF16)
        s = _dot_nt(kk[:, ns], qn) + bias
        sink = sink_ref[n]
        m = jnp.maximum(jnp.max(s, axis=0, keepdims=True), sink)
        p = jnp.exp(s - m)
        den = jnp.sum(p, axis=0, keepdims=True) + jnp.exp(sink - m)
        o_t = _dot(vt[ns, :], p.astype(BF16)) / den
        outs += [o_t[:, g * W:(g + 1) * W].T for g in range(G)]
    y_ref[...] = jnp.concatenate(outs, axis=1)


def _swa_prompt(q, k, v, sinks, *, nseq, seqlen):
    W = WINDOW
    T = q.shape[0]
    nb = seqlen // W
    nq, nk = SWA_HEADS * HEAD_DIM, SWA_KV_HEADS * HEAD_DIM
    cur = lambda b, i: (b * nb + i, 0)
    prev = lambda b, i: (b * nb + jnp.maximum(i - 1, 0), 0)
    cur_t = lambda b, i: (0, b * nb + i)
    prev_t = lambda b, i: (0, b * nb + jnp.maximum(i - 1, 0))
    vt = v.T
    return pl.pallas_call(
        _swa_prompt_kernel,
        grid=(nseq, nb),
        in_specs=[pl.BlockSpec((W, nq), cur),
                  pl.BlockSpec((W, nk), prev), pl.BlockSpec((W, nk), cur),
                  pl.BlockSpec((nk, W), prev_t), pl.BlockSpec((nk, W), cur_t),
                  pl.BlockSpec(sinks.shape, lambda b, i: (0, 0, 0))],
        out_specs=pl.BlockSpec((W, nq), cur),
        out_shape=jax.ShapeDtypeStruct((T, nq), F32),
        compiler_params=_cparams(("parallel", "parallel")),
        name="swa_prompt",
    )(q, k, k, vt, vt, sinks)


def _swa_sample_kernel(qb_ref, kn_ref, vn_ref, kt_ref, vt_ref, sink_ref, o_ref, kto_ref, vto_ref, *, Q):
    W = WINDOW
    R = SWA_HEADS * Q
    KD = SWA_KV_HEADS * HEAD_DIM
    neg_inf = F32(-jnp.inf)
    lane = lax.broadcasted_iota(I32, (KD, W), 1)

    def new_cols_t(x_ref):
        pad = jnp.concatenate([x_ref[...], jnp.zeros((W - Q, KD), F32)], axis=0)
        return pltpu.roll(pad.T, W - Q, axis=1)

    kt = kt_ref[0]
    vt = vt_ref[0]
    knt = new_cols_t(kn_ref)
    vnt = new_cols_t(vn_ref)
    kto_ref[0] = jnp.where(lane >= W - Q, knt, pltpu.roll(kt, W - Q, axis=1))
    vto_ref[0] = jnp.where(lane >= W - Q, vnt, pltpu.roll(vt, W - Q, axis=1))

    qb = (qb_ref[0] * (HEAD_DIM ** -0.5)).astype(BF16)
    qi = lax.broadcasted_iota(I32, (R, W), 0) % Q
    cc = lax.broadcasted_iota(I32, (R, W), 1)
    s_buf = jnp.where(cc > qi, _dot(qb, kt.astype(BF16)), neg_inf)
    s_new = jnp.where((cc >= W - Q) & (cc - (W - Q) <= qi), _dot(qb, knt.astype(BF16)), neg_inf)
    sink = sink_ref[...]
    m = jnp.maximum(jnp.maximum(jnp.max(s_buf, axis=-1, keepdims=True),
                                jnp.max(s_new, axis=-1, keepdims=True)), sink)
    p_buf = jnp.exp(s_buf - m)
    p_new = jnp.exp(s_new - m)
    den = jnp.sum(p_buf, axis=-1, keepdims=True) + jnp.sum(p_new, axis=-1, keepdims=True) + jnp.exp(sink - m)
    o = (_dot_nt(p_buf.astype(BF16), vt.astype(BF16)) + _dot_nt(p_new.astype(BF16), vnt.astype(BF16))) / den
    rpk = SWA_GROUP * Q
    o_ref[0] = jnp.concatenate(
        [o[n * rpk:(n + 1) * rpk, n * HEAD_DIM:(n + 1) * HEAD_DIM] for n in range(SWA_KV_HEADS)], axis=0)


def _swa_sample(qblk, k_new, v_new, kt, vt, sink_col, *, nseq, Q):
    W = WINDOW
    R = SWA_HEADS * Q
    KD = SWA_KV_HEADS * HEAD_DIM
    return pl.pallas_call(
        functools.partial(_swa_sample_kernel, Q=Q),
        grid=(nseq,),
        in_specs=[pl.BlockSpec((1, R, KD), lambda b: (b, 0, 0)),
                  pl.BlockSpec((Q, KD), lambda b: (b, 0)),
                  pl.BlockSpec((Q, KD), lambda b: (b, 0)),
                  pl.BlockSpec((1, KD, W), lambda b: (b, 0, 0)),
                  pl.BlockSpec((1, KD, W), lambda b: (b, 0, 0)),
                  pl.BlockSpec((R, 1), lambda b: (0, 0))],
        out_specs=[pl.BlockSpec((1, R, HEAD_DIM), lambda b: (b, 0, 0)),
                   pl.BlockSpec((1, KD, W), lambda b: (b, 0, 0)),
                   pl.BlockSpec((1, KD, W), lambda b: (b, 0, 0))],
        out_shape=[jax.ShapeDtypeStruct((nseq, R, HEAD_DIM), F32),
                   jax.ShapeDtypeStruct((nseq, KD, W), F32),
                   jax.ShapeDtypeStruct((nseq, KD, W), F32)],
        compiler_params=_cparams(("parallel",)),
        name="swa_sample",
    )(qblk, k_new, v_new, kt, vt, sink_col)


def _fetch_pages(pt_ref, srcs, bufs, sem, *, layer, n_pages):
    P = PAGE_SIZE
    b = pl.program_id(0)
    nb = pl.num_programs(0)

    def copies(bb, slot):
        out = []
        for p in range(n_pages):
            page = pt_ref[bb, p]
            for j, (src, buf) in enumerate(zip(srcs, bufs)):
                out.append(pltpu.make_async_copy(src.at[layer, page], buf.at[slot, :, p * P:(p + 1) * P],
                                                 sem.at[slot, j]))
        return out

    slot = b % 2

    @pl.when(b == 0)
    def _():
        for cp in copies(0, 0):
            cp.start()

    @pl.when(b + 1 < nb)
    def _():
        for cp in copies(b + 1, 1 - slot):
            cp.start()

    for cp in copies(b, slot):
        cp.wait()
    return slot


def _dsa_sample_scores_kernel(pt_ref, ia_ref, iwc_ref, ikn_ref, ci_ref, key_ref, ibuf, sem, *, layer, n_pages, Q):
    P = PAGE_SIZE
    past = n_pages * P
    neg_inf = F32(-jnp.inf)
    slot = _fetch_pages(pt_ref, [ci_ref], [ibuf], sem, layer=layer, n_pages=n_pages)

    ia = ia_ref[0].astype(BF16)
    iwc = iwc_ref[0] * (IDX_HEADS ** -0.5 * IDX_DIM ** -0.5)

    def head_sum(d):
        d = jnp.maximum(d, 0.0) * iwc
        acc = d[0:Q]
        for h in range(1, IDX_HEADS):
            acc = acc + d[h * Q:(h + 1) * Q]
        return acc

    sc_past = head_sum(_dot(ia, ibuf[slot].astype(BF16)))
    ikn = jnp.concatenate([ikn_ref[...], jnp.zeros((P - Q, IDX_DIM), F32)], axis=0)
    sc_new = head_sum(_dot_nt(ia, ikn.astype(BF16)))
    qrow = lax.broadcasted_iota(I32, (Q, P), 0)
    lane = lax.broadcasted_iota(I32, (Q, P), 1)
    sc_new = jnp.where(lane <= qrow, sc_new, neg_inf)
    key_ref[0, :, 0:past] = _float_key(_pos_zero(sc_past))
    key_ref[0, :, past:past + P] = _float_key(_pos_zero(sc_new))


def _dsa_sample_select_kernel(key_ref, bias_ref, *, topk):
    GB, Q, N = key_ref.shape
    R = GB * Q
    keys = key_ref[...].reshape(R, N)
    idx = lax.broadcasted_iota(I32, (R, N), 1)

    def lane_count(mask):
        mi = mask.astype(I32)
        part = mi[:, 0:LANES]
        for t in range(1, N // LANES):
            part = part + mi[:, t * LANES:(t + 1) * LANES]
        return jnp.sum(part, axis=1, keepdims=True)

    def search_bit(p, thr):
        cand = thr + jnp.left_shift(I32(1), 31 - p)
        return jnp.where(lane_count(keys >= cand) >= topk, cand, thr)

    thr = lax.fori_loop(0, 32, search_bit, jnp.full((R, 1), INT_MIN, I32))

    need = topk - lane_count(keys > thr)
    n_eq = lane_count(keys == thr)
    excess = (n_eq > need) & (thr > KEY_NEG_INF)
    nbits = int(N).bit_length()

    def search_idx(p, lim):
        cand = lim + jnp.left_shift(I32(1), nbits - 1 - p)
        return jnp.where(lane_count((keys == thr) & (idx < cand)) < need, cand, lim)

    lim = lax.cond(jnp.max(excess.astype(I32)) > 0,
                   lambda: lax.fori_loop(0, nbits, search_idx, jnp.zeros((R, 1), I32)),
                   lambda: jnp.zeros((R, 1), I32))
    last_tie = jnp.where(excess, lim, N)
    sel = ((keys > thr) | ((keys == thr) & (idx <= last_tie))) & (keys > KEY_NEG_INF)
    bias_ref[...] = jnp.where(sel, 0.0, F32(-jnp.inf)).reshape(GB, Q, N)


def _dsa_sample_attend_kernel(pt_ref, bias_ref, qb_ref, kn_ref, vn_ref, ck_ref, cv_ref, o_ref,
                              kbuf, vbuf, sem, *, layer, n_pages, Q):
    P = PAGE_SIZE
    past = n_pages * P
    KD = DSA_KV_HEADS * HEAD_DIM
    slot = _fetch_pages(pt_ref, [ck_ref, cv_ref], [kbuf, vbuf], sem, layer=layer, n_pages=n_pages)
    bias_past = bias_ref[0, :, 0:past]
    bias_new = bias_ref[0, :, past:past + P]

    qb = (qb_ref[0] * (HEAD_DIM ** -0.5)).astype(BF16)
    kn = jnp.concatenate([kn_ref[...], jnp.zeros((P - Q, KD), F32)], axis=0).astype(BF16)
    vn = jnp.concatenate([vn_ref[...], jnp.zeros((P - Q, KD), F32)], axis=0).astype(BF16)
    s_past = _dot(qb, kbuf[slot].astype(BF16)) + jnp.concatenate([bias_past] * DSA_HEADS, axis=0)
    s_new = _dot_nt(qb, kn) + jnp.concatenate([bias_new] * DSA_HEADS, axis=0)
    m = jnp.maximum(jnp.max(s_past, axis=-1, keepdims=True), jnp.max(s_new, axis=-1, keepdims=True))
    p_past = jnp.exp(s_past - m)
    p_new = jnp.exp(s_new - m)
    den = jnp.sum(p_past, axis=-1, keepdims=True) + jnp.sum(p_new, axis=-1, keepdims=True)
    o = (_dot_nt(p_past.astype(BF16), vbuf[slot].astype(BF16)) + _dot(p_new.astype(BF16), vn)) / den
    rpk = DSA_GROUP * Q
    o_ref[0] = jnp.concatenate(
        [o[n * rpk:(n + 1) * rpk, n * HEAD_DIM:(n + 1) * HEAD_DIM] for n in range(DSA_KV_HEADS)], axis=0)


def _dsa_sample(page_table, ia, iwc, ik_new, qblk, k_new, v_new, ci_t, ck_t, cv_t, *, layer, nseq, Q):
    P = PAGE_SIZE
    n_pages = page_table.shape[1]
    past = n_pages * P
    KD = DSA_KV_HEADS * HEAD_DIM
    R = DSA_HEADS * Q
    RI = IDX_HEADS * Q
    topk = min(DSA_TOPK, (past + Q) // 4)
    N = past + P
    keys = pl.pallas_call(
        functools.partial(_dsa_sample_scores_kernel, layer=layer, n_pages=n_pages, Q=Q),
        grid_spec=pltpu.PrefetchScalarGridSpec(
            num_scalar_prefetch=1,
            grid=(nseq,),
            in_specs=[pl.BlockSpec((1, RI, IDX_DIM), lambda b, pt: (b, 0, 0)),
                      pl.BlockSpec((1, RI, 1), lambda b, pt: (b, 0, 0)),
                      pl.BlockSpec((Q, IDX_DIM), lambda b, pt: (b, 0)),
                      pl.BlockSpec(memory_space=pl.ANY)],
            out_specs=pl.BlockSpec((1, Q, N), lambda b, pt: (b, 0, 0)),
            scratch_shapes=[pltpu.VMEM((2, IDX_DIM, past), F32),
                            pltpu.SemaphoreType.DMA((2, 1))]),
        out_shape=jax.ShapeDtypeStruct((nseq, Q, N), I32),
        compiler_params=_cparams(("arbitrary",)),
        name="dsa_sample_scores",
    )(page_table, ia, iwc, ik_new, ci_t)

    gb = 8 if nseq % 8 == 0 else 1
    bias = pl.pallas_call(
        functools.partial(_dsa_sample_select_kernel, topk=topk),
        grid=(nseq // gb,),
        in_specs=[pl.BlockSpec((gb, Q, N), lambda g: (g, 0, 0))],
        out_specs=pl.BlockSpec((gb, Q, N), lambda g: (g, 0, 0)),
        out_shape=jax.ShapeDtypeStruct((nseq, Q, N), F32),
        compiler_params=_cparams(("parallel",)),
        name="dsa_sample_select",
    )(keys)

    return pl.pallas_call(
        functools.partial(_dsa_sample_attend_kernel, layer=layer, n_pages=n_pages, Q=Q),
        grid_spec=pltpu.PrefetchScalarGridSpec(
            num_scalar_prefetch=1,
            grid=(nseq,),
            in_specs=[pl.BlockSpec((1, Q, N), lambda b, pt: (b, 0, 0)),
                      pl.BlockSpec((1, R, KD), lambda b, pt: (b, 0, 0)),
                      pl.BlockSpec((Q, KD), lambda b, pt: (b, 0)),
                      pl.BlockSpec((Q, KD), lambda b, pt: (b, 0)),
                      pl.BlockSpec(memory_space=pl.ANY),
                      pl.BlockSpec(memory_space=pl.ANY)],
            out_specs=pl.BlockSpec((1, R, HEAD_DIM), lambda b, pt: (b, 0, 0)),
            scratch_shapes=[pltpu.VMEM((2, KD, past), F32),
                            pltpu.VMEM((2, KD, past), F32),
                            pltpu.SemaphoreType.DMA((2, 2))]),
        out_shape=jax.ShapeDtypeStruct((nseq, R, HEAD_DIM), F32),
        compiler_params=_cparams(("arbitrary",)),
        name="dsa_sample_attend",
    )(page_table, bias, qblk, k_new, v_new, ck_t, cv_t)


def _even_weight(w):
    offs = np.cumsum((0,) + EVEN_COLS)
    seg = lambda j: w[:, offs[j]:offs[j + 1]]
    gq, gk, gv, gr, gg, dq, dk, dv, iq, ik, iw = [seg(j) for j in range(len(EVEN_COLS))]
    pad = jnp.zeros((w.shape[0], SMALL_COLS - IDX_DIM - GLA_GATE_RANK - IDX_HEADS), w.dtype)
    return jnp.concatenate([gq, gk, gv, gr, dq, iq, dk, dv, ik, gg, iw, pad], axis=1).astype(BF16)


def _rot_cols(w, nheads):
    half = HEAD_DIM // 2
    w4 = w.reshape(w.shape[0], nheads, 2, half)
    return jnp.concatenate([-w4[:, :, 1], w4[:, :, 0]], axis=-1).reshape(w.shape[0], nheads * HEAD_DIM)


def _odd_weight(w):
    nq, nk = SWA_HEADS * HEAD_DIM, SWA_KV_HEADS * HEAD_DIM
    wq, wk = w[:, :nq], w[:, nq:nq + nk]
    return jnp.concatenate([w, _rot_cols(wq, SWA_HEADS), _rot_cols(wk, SWA_KV_HEADS)], axis=1).astype(BF16)


def _rope_tables(pos, reps):
    half = HEAD_DIM // 2
    inv = ROPE_THETA ** (-jnp.arange(half, dtype=F32) / half)
    ang = pos.astype(F32)[:, None] * inv[None, :]
    cos = jnp.tile(jnp.cos(ang), (reps, 2 * LANES // HEAD_DIM))
    sin = jnp.tile(jnp.sin(ang), (reps, 2 * LANES // HEAD_DIM))
    return cos, sin


def _block_diag_queries(q, nseq, Q, n_kv, group):
    q5 = q.reshape(nseq, Q, n_kv, group, HEAD_DIM).transpose(0, 2, 3, 1, 4)
    eye = jnp.eye(n_kv, dtype=q.dtype)
    blk = q5[:, :, :, :, None, :] * eye[None, :, None, None, :, None]
    return blk.reshape(nseq, n_kv * group * Q, n_kv * HEAD_DIM)


def _rows_to_tokens(o, nseq, Q, heads):
    return o.reshape(nseq, heads, Q, HEAD_DIM).transpose(0, 2, 1, 3).reshape(nseq * Q, heads * HEAD_DIM)


def kernel(x_prompt, x_sample, cache_k, cache_v, cache_idx, state_gla, state_swa_k, state_swa_v, page_table,
           w_in_even, gla_gate_w2, gla_gate_b, gla_norm_g, w_out_even, w_in_odd, swa_sinks, w_out_odd,
           ffn_w_gu, ffn_w_down, ln_g, ln_b):
    B, L, D = x_prompt.shape
    Bd, Q, _ = x_sample.shape
    depth = ffn_w_gu.shape[0]
    d_ff = ffn_w_down.shape[1]
    n_even, n_pool = cache_k.shape[0], cache_k.shape[1]
    alpha = (2.0 * depth) ** 0.25
    tm = 512 if (B * L) % 512 == 0 and (Bd * Q) % 512 == 0 else 128
    ff_chunk = 256

    w_even = [_even_weight(w_in_even[i]) for i in range(n_even)]
    w_odd = [_odd_weight(w_in_odd[i]) for i in range(w_in_odd.shape[0])]
    gla_rows = GLA_HEADS * GLA_DV
    wo_even = [(w_out_even[i, :gla_rows].astype(BF16), w_out_even[i, gla_rows:].astype(BF16)) for i in range(n_even)]
    wo_odd = [w_out_odd[i].astype(BF16) for i in range(w_out_odd.shape[0])]
    w_gu = [ffn_w_gu[l].astype(BF16) for l in range(depth)]
    w_dn = [ffn_w_down[l].astype(BF16) for l in range(depth)]
    w2 = [gla_gate_w2[i].astype(BF16) for i in range(n_even)]

    ci_t = jnp.swapaxes(cache_idx, 2, 3)
    kd = DSA_KV_HEADS * HEAD_DIM
    ck_t = cache_k.transpose(0, 1, 3, 4, 2).reshape(n_even, n_pool, kd, PAGE_SIZE)
    cv_t = cache_v.transpose(0, 1, 3, 4, 2).reshape(n_even, n_pool, kd, PAGE_SIZE)
    skd = SWA_KV_HEADS * HEAD_DIM
    n_odd = state_swa_k.shape[0]
    swk_t = state_swa_k.transpose(0, 1, 3, 4, 2).reshape(n_odd, Bd, skd, WINDOW)
    swv_t = state_swa_v.transpose(0, 1, 3, 4, 2).reshape(n_odd, Bd, skd, WINDOW)

    cos_p, sin_p = _rope_tables(jnp.arange(L), 1)
    past_len = page_table.shape[1] * PAGE_SIZE
    cos_s, sin_s = _rope_tables(past_len + jnp.arange(Q), Bd)

    def finish_layer(l, x, lhs, ws):
        h = _outproj_ln(x, lhs, ws, ln_g[l, 0][None], ln_b[l, 0][None], alpha, tm)
        return _ffn_ln(h, w_gu[l], w_dn[l], ln_g[l, 1][None], ln_b[l, 1][None], alpha, tm, ff_chunk)

    def even_common(i, x):
        return _project(x, w_even[i], EVEN_SPLITS, tm)

    x = x_prompt.reshape(B * L, D)
    ev_p, od_p = [], []
    for l in range(depth):
        i = l // 2
        if l % 2 == 0:
            gla, dq, iq, dk, dv, small = even_common(i, x)
            s0 = jnp.zeros((B, GLA_HEADS, GLA_DK, GLA_DV), F32)
            y_gla, s_fin = _gla(gla, small, s0, w2[i], gla_gate_b[i][None], gla_norm_g[i][None],
                                nseq=B, seqlen=L, C=64, TB=256)
            iw_t = small[:, SM_IW:SM_IW + IDX_HEADS].T
            y_dsa = _dsa_prompt(dq, iq, iw_t, small, dk, dv, nseq=B, seqlen=L)
            ev_p.append((dk.reshape(B, L, DSA_KV_HEADS, HEAD_DIM), dv.reshape(B, L, DSA_KV_HEADS, HEAD_DIM),
                         small[:, SM_IK:SM_IK + IDX_DIM].reshape(B, L, IDX_DIM), s_fin))
            x = finish_layer(l, x, [y_gla, y_dsa], list(wo_even[i]))
        else:
            q, k, v = _project_rope(x, w_odd[i], cos_p, sin_p, tm)
            sink_rows = jnp.repeat(swa_sinks[i], WINDOW).reshape(SWA_KV_HEADS, 1, SWA_GROUP * WINDOW)
            y = _swa_prompt(q, k, v, sink_rows, nseq=B, seqlen=L)
            k4 = k.reshape(B, L, SWA_KV_HEADS, HEAD_DIM)
            v4 = v.reshape(B, L, SWA_KV_HEADS, HEAD_DIM)
            od_p.append((k4[:, L - WINDOW:], v4[:, L - WINDOW:]))
            x = finish_layer(l, x, [y], [wo_odd[i]])
    y_prompt = x.reshape(B, L, D)

    x = x_sample.reshape(Bd * Q, D)
    ev_s, od_s = [], []
    for l in range(depth):
        i = l // 2
        if l % 2 == 0:
            gla, dq, iq, dk, dv, small = even_common(i, x)
            y_gla, s_fin = _gla(gla, small, state_gla[i], w2[i], gla_gate_b[i][None], gla_norm_g[i][None],
                                nseq=Bd, seqlen=Q, C=Q, TB=Q)
            ik_new = small[:, SM_IK:SM_IK + IDX_DIM]
            iw = small[:, SM_IW:SM_IW + IDX_HEADS]
            ia = iq.reshape(Bd, Q, IDX_HEADS, IDX_DIM).transpose(0, 2, 1, 3).reshape(Bd, IDX_HEADS * Q, IDX_DIM)
            iwc = iw.reshape(Bd, Q, IDX_HEADS).transpose(0, 2, 1).reshape(Bd, IDX_HEADS * Q, 1)
            qblk = _block_diag_queries(dq, Bd, Q, DSA_KV_HEADS, DSA_GROUP)
            o = _dsa_sample(page_table, ia, iwc, ik_new, qblk, dk, dv, ci_t, ck_t, cv_t, layer=i, nseq=Bd, Q=Q)
            y_dsa = _rows_to_tokens(o, Bd, Q, DSA_HEADS)
            ev_s.append((dk.reshape(Bd, Q, DSA_KV_HEADS, HEAD_DIM), dv.reshape(Bd, Q, DSA_KV_HEADS, HEAD_DIM),
                         ik_new.reshape(Bd, Q, IDX_DIM), s_fin))
            x = finish_layer(l, x, [y_gla, y_dsa], list(wo_even[i]))
        else:
            q, k, v = _project_rope(x, w_odd[i], cos_s, sin_s, tm)
            qblk = _block_diag_queries(q, Bd, Q, SWA_KV_HEADS, SWA_GROUP)
            sink_col = jnp.repeat(swa_sinks[i], Q)[:, None]
            o, kt_new, vt_new = _swa_sample(qblk, k, v, swk_t[i], swv_t[i], sink_col, nseq=Bd, Q=Q)
            y = _rows_to_tokens(o, Bd, Q, SWA_HEADS)
            back = lambda t: t.reshape(Bd, SWA_KV_HEADS, HEAD_DIM, WINDOW).transpose(0, 3, 1, 2)
            od_s.append((back(kt_new), back(vt_new)))
            x = finish_layer(l, x, [y], [wo_odd[i]])
    y_sample = x.reshape(Bd, Q, D)

    stack = lambda states: [jnp.stack(z) for z in zip(*states)]
    k_p, v_p, idx_p, gla_p = stack(ev_p)
    swk_p, swv_p = stack(od_p)
    k_s, v_s, idx_s, gla_s = stack(ev_s)
    swk_s, swv_s = stack(od_s)
    return (y_prompt, y_sample, k_p, v_p, idx_p, gla_p, swk_p, swv_p, k_s, v_s, idx_s, gla_s, swk_s, swv_s)
```

```python
import functools

import jax
import jax.numpy as jnp
import numpy as np
from jax import lax
from jax.experimental import pallas as pl
from jax.experimental.pallas import tpu as pltpu

F32 = jnp.float32
BF16 = jnp.bfloat16
I32 = jnp.int32

GLA_HEADS = 4
GLA_DK = 64
GLA_DV = 128
GLA_GATE_RANK = 16
GLA_TAU = 16.0
DSA_HEADS = 8
DSA_KV_HEADS = 2
DSA_GROUP = DSA_HEADS // DSA_KV_HEADS
HEAD_DIM = 64
IDX_HEADS = 8
IDX_DIM = 64
DSA_TOPK = 256
DSA_QBLOCK = 128
SWA_HEADS = 16
SWA_KV_HEADS = 4
SWA_GROUP = SWA_HEADS // SWA_KV_HEADS
WINDOW = 128
ROPE_THETA = 150000.0
PAGE_SIZE = 128
LN_EPS = 1e-5
RMS_EPS = 1e-6

EVEN_COLS = (GLA_HEADS * GLA_DK, GLA_HEADS * GLA_DK, GLA_HEADS * GLA_DV, GLA_HEADS * GLA_DV, GLA_GATE_RANK,
             DSA_HEADS * HEAD_DIM, DSA_KV_HEADS * HEAD_DIM, DSA_KV_HEADS * HEAD_DIM,
             IDX_HEADS * IDX_DIM, IDX_DIM, IDX_HEADS)
GLA_COLS = 2 * GLA_HEADS * GLA_DK + 2 * GLA_HEADS * GLA_DV
DQ_COLS = DSA_HEADS * HEAD_DIM
IQ_COLS = IDX_HEADS * IDX_DIM
DKV_COLS = DSA_KV_HEADS * HEAD_DIM
SMALL_COLS = 128
SM_IK = 0
SM_GG = IDX_DIM
SM_IW = IDX_DIM + GLA_GATE_RANK
EVEN_SPLITS = (GLA_COLS, DQ_COLS, IQ_COLS, DKV_COLS, DKV_COLS, SMALL_COLS)

LANES = 128
SUBLANES = 8
VMEM_LIMIT_BYTES = 56 * 1024 * 1024

I16 = jnp.int16
I16_MIN = -(2 ** 15)
PACK16 = 2 * SUBLANES
ONES_ROWS = PACK16
LOG2E = 1.4426950408889634
INT_MIN = np.int32(-(2 ** 31))
KEY_NEG_INF = np.int32(np.array(-np.inf, np.float32).view(np.int32) ^ np.int32(0x7FFFFFFF))
GLA_EXP_CLAMP = 80.0


def _cparams(sem):
    return pltpu.CompilerParams(dimension_semantics=sem, vmem_limit_bytes=VMEM_LIMIT_BYTES)


def _dot(a, b):
    return jnp.dot(a, b, preferred_element_type=F32)


def _dot_nt(a, b):
    return lax.dot_general(a, b, (((1,), (1,)), ((), ())), preferred_element_type=F32)


def _dot_tn(a, b):
    return lax.dot_general(a, b, (((0,), (0,)), ((), ())), preferred_element_type=F32)


def _layer_norm(v, g, b):
    mu = jnp.mean(v, axis=-1, keepdims=True)
    c = v - mu
    var = jnp.mean(c * c, axis=-1, keepdims=True)
    return c * lax.rsqrt(var + LN_EPS) * g + b


def _silu(v):
    return v * jax.nn.sigmoid(v)


def _tree_sum(xs):
    while len(xs) > 1:
        xs = [xs[j] + xs[j + 1] for j in range(0, len(xs) - 1, 2)] + ([xs[-1]] if len(xs) % 2 else [])
    return xs[0]


def _pos_zero(s):
    return jnp.where(s == 0.0, 0.0, s)


def _float_key(s):
    bits = lax.bitcast_convert_type(s, I32)
    return jnp.where(bits < 0, bits ^ np.int32(0x7FFFFFFF), bits)


def _key_hi16(key):
    return (key >> 16).astype(I16)


def _key_lo16(key):
    return ((key & 0xFFFF) + I16_MIN).astype(I16)


def _proj_kernel(x_ref, w_ref, *o_refs, splits):
    x = x_ref[...].astype(BF16)
    off = 0
    for o_ref, n in zip(o_refs, splits):
        o_ref[...] = _dot(x, w_ref[:, off:off + n])
        off += n


def _project(x, w, splits, tm):
    T, D = x.shape
    N = w.shape[1]
    return pl.pallas_call(
        functools.partial(_proj_kernel, splits=splits),
        grid=(T // tm,),
        in_specs=[pl.BlockSpec((tm, D), lambda i: (i, 0)),
                  pl.BlockSpec((D, N), lambda i: (0, 0))],
        out_specs=[pl.BlockSpec((tm, n), lambda i: (i, 0)) for n in splits],
        out_shape=[jax.ShapeDtypeStruct((T, n), F32) for n in splits],
        compiler_params=_cparams(("parallel",)),
        name="proj_even",
    )(x, w)


def _proj_rope_kernel(x_ref, w_ref, cos_ref, sin_ref, q_ref, k_ref, v_ref, *, nq, nk):
    x = x_ref[...].astype(BF16)
    cos = cos_ref[...]
    sin = sin_ref[...]
    half = HEAD_DIM // 2
    first = lax.broadcasted_iota(I32, cos.shape, 1) % HEAD_DIM < half

    def rope(u):
        swapped = jnp.where(first, pltpu.roll(u, LANES - half, axis=1), pltpu.roll(u, half, axis=1))
        return u * cos + swapped * sin

    for j in range(nq // LANES):
        q_ref[:, j * LANES:(j + 1) * LANES] = rope(_dot(x, w_ref[:, j * LANES:(j + 1) * LANES]))
    for j in range(nk // LANES):
        k_ref[:, j * LANES:(j + 1) * LANES] = rope(_dot(x, w_ref[:, nq + j * LANES:nq + (j + 1) * LANES]))
    v_ref[...] = _dot(x, w_ref[:, nq + nk:nq + 2 * nk])


def _project_rope(x, w, layer, cos, sin, tm):
    T, D = x.shape
    N = w.shape[2]
    nq, nk = SWA_HEADS * HEAD_DIM, SWA_KV_HEADS * HEAD_DIM
    ntab = cos.shape[0] // tm
    return pl.pallas_call(
        functools.partial(_proj_rope_kernel, nq=nq, nk=nk),
        grid=(T // tm,),
        in_specs=[pl.BlockSpec((tm, D), lambda i: (i, 0)),
                  pl.BlockSpec((None, D, N), lambda i: (layer, 0, 0)),
                  pl.BlockSpec((tm, LANES), lambda i: (i % ntab, 0)),
                  pl.BlockSpec((tm, LANES), lambda i: (i % ntab, 0))],
        out_specs=[pl.BlockSpec((tm, nq), lambda i: (i, 0)),
                   pl.BlockSpec((tm, nk), lambda i: (i, 0)),
                   pl.BlockSpec((tm, nk), lambda i: (i, 0))],
        out_shape=[jax.ShapeDtypeStruct((T, nq), F32),
                   jax.ShapeDtypeStruct((T, nk), F32),
                   jax.ShapeDtypeStruct((T, nk), F32)],
        compiler_params=_cparams(("parallel",)),
        name="proj_odd_rope",
    )(x, w, cos, sin)


def _outproj_ln_kernel(*refs, n_lhs, alpha):
    x_ref = refs[0]
    lhs = refs[1:1 + n_lhs]
    ws = refs[1 + n_lhs:1 + 2 * n_lhs]
    g_ref, b_ref, o_ref = refs[1 + 2 * n_lhs:]
    acc = alpha * x_ref[...]
    for a_ref, w_ref in zip(lhs, ws):
        acc = acc + _dot(a_ref[...].astype(BF16), w_ref[...])
    o_ref[...] = _layer_norm(acc, g_ref[...], b_ref[...])


def _outproj_ln(x, lhs, w, layer, g, b, alpha, tm):
    T, D = x.shape
    n = len(lhs)
    in_specs = [pl.BlockSpec((tm, D), lambda i: (i, 0))]
    in_specs += [pl.BlockSpec((tm, a.shape[1]), lambda i: (i, 0)) for a in lhs]
    in_specs += [pl.BlockSpec((None, a.shape[1], D), lambda i, r=r: (layer, r, 0)) for r, a in enumerate(lhs)]
    in_specs += [pl.BlockSpec((1, D), lambda i: (0, 0))] * 2
    ws = [w] * n
    return pl.pallas_call(
        functools.partial(_outproj_ln_kernel, n_lhs=n, alpha=alpha),
        grid=(T // tm,),
        in_specs=in_specs,
        out_specs=pl.BlockSpec((tm, D), lambda i: (i, 0)),
        out_shape=jax.ShapeDtypeStruct((T, D), F32),
        compiler_params=_cparams(("parallel",)),
        name="outproj_ln",
    )(x, *lhs, *ws, g, b)


def _ffn_ln_kernel(h_ref, wgu_ref, wd_ref, g_ref, b_ref, o_ref, *, dff, chunk, alpha):
    h = h_ref[...]
    hb = h.astype(BF16)
    acc = alpha * h
    for c in range(dff // chunk):
        gate = _dot(hb, wgu_ref[:, c * chunk:(c + 1) * chunk])
        up = _dot(hb, wgu_ref[:, dff + c * chunk:dff + (c + 1) * chunk])
        act = (_silu(gate) * up).astype(BF16)
        acc = acc + _dot(act, wd_ref[c * chunk:(c + 1) * chunk, :])
    o_ref[...] = _layer_norm(acc, g_ref[...], b_ref[...])


def _ffn_ln(h, wgu, wd, layer, g, b, alpha, tm, chunk):
    T, D = h.shape
    dff = wd.shape[1]
    return pl.pallas_call(
        functools.partial(_ffn_ln_kernel, dff=dff, chunk=chunk, alpha=alpha),
        grid=(T // tm,),
        in_specs=[pl.BlockSpec((tm, D), lambda i: (i, 0)),
                  pl.BlockSpec((None,) + wgu.shape[1:], lambda i: (layer, 0, 0)),
                  pl.BlockSpec((None,) + wd.shape[1:], lambda i: (layer, 0, 0)),
                  pl.BlockSpec((1, D), lambda i: (0, 0)),
                  pl.BlockSpec((1, D), lambda i: (0, 0))],
        out_specs=pl.BlockSpec((tm, D), lambda i: (i, 0)),
        out_shape=jax.ShapeDtypeStruct((T, D), F32),
        compiler_params=_cparams(("parallel",)),
        name="ffn_ln",
    )(h, wgu, wd, g, b)


def _gla_kernel(gla_ref, small_ref, s0_ref, w2_ref, gb_ref, ng_ref, y_ref, sout_ref, s_scr, *, C, TB):
    H, DK, DV = GLA_HEADS, GLA_DK, GLA_DV
    j = pl.program_id(1)

    @pl.when(j == 0)
    def _():
        s_scr[...] = s0_ref[0]

    rowc = lax.broadcasted_iota(I32, (C, C), 0)
    colc = lax.broadcasted_iota(I32, (C, C), 1)
    tril = rowc >= colc
    tril_b = tril.astype(BF16)
    eye_dk = lax.broadcasted_iota(I32, (DK, DK), 0) == lax.broadcasted_iota(I32, (DK, DK), 1)
    w2 = w2_ref[...]
    gb = gb_ref[...]
    ng = ng_ref[...]
    mid = C // 2 - 1

    for c in range(TB // C):
        rows = slice(c * C, (c + 1) * C)
        gg = small_ref[rows, SM_GG:SM_GG + GLA_GATE_RANK].astype(BF16)
        z = _dot(gg, w2) + gb
        glog = -(jnp.maximum(-z, 0.0) + jnp.log1p(jnp.exp(-jnp.abs(z)))) * (1.0 / GLA_TAU)
        g1 = glog.astype(BF16)
        r1 = glog - g1.astype(F32)
        g2 = r1.astype(BF16)
        g3 = (r1 - g2.astype(F32)).astype(BF16)
        b = _dot(tril_b, g1) + _dot(tril_b, g2) + _dot(tril_b, g3)
        for h in range(H):
            qh = gla_ref[rows, h * DK:(h + 1) * DK] * (DK ** -0.5)
            kh = gla_ref[rows, H * DK + h * DK:H * DK + (h + 1) * DK]
            vh = gla_ref[rows, 2 * H * DK + h * DV:2 * H * DK + (h + 1) * DV].astype(BF16)
            rh = gla_ref[rows, 2 * H * DK + H * DV + h * DV:2 * H * DK + H * DV + (h + 1) * DV]
            bh = b[:, h * DK:(h + 1) * DK]
            bl = bh[C - 1:C, :]
            rr = bh[mid:mid + 1, :]
            qt = (qh * jnp.exp(jnp.clip(bh - rr, -GLA_EXP_CLAMP, GLA_EXP_CLAMP))).astype(BF16)
            kt = (kh * jnp.exp(jnp.clip(rr - bh, -GLA_EXP_CLAMP, GLA_EXP_CLAMP))).astype(BF16)
            a = jnp.where(tril, _dot_nt(qt, kt), 0.0)
            s_prev = s_scr[h]
            o = _dot(a.astype(BF16), vh) + _dot((qh * jnp.exp(bh)).astype(BF16), s_prev.astype(BF16))
            kin = (kh * jnp.exp(bl - bh)).astype(BF16)
            kv = _dot_tn(kin, vh)
            dcol = jnp.sum(jnp.where(eye_dk, jnp.broadcast_to(bl, (DK, DK)), 0.0), axis=1, keepdims=True)
            s_scr[h] = jnp.exp(dcol) * s_prev + kv
            ms = jnp.mean(o * o, axis=-1, keepdims=True)
            y_ref[rows, h * DV:(h + 1) * DV] = o * lax.rsqrt(ms + RMS_EPS) * ng * _silu(rh)

    @pl.when(j == pl.num_programs(1) - 1)
    def _():
        sout_ref[0] = s_scr[...]


def _gla(gla, small, s0, s0_layer, w2, layer, gb, ng, *, nseq, seqlen, C, TB):
    H, DK, DV = GLA_HEADS, GLA_DK, GLA_DV
    T = gla.shape[0]
    nj = seqlen // TB
    return pl.pallas_call(
        functools.partial(_gla_kernel, C=C, TB=TB),
        grid=(nseq, nj),
        in_specs=[pl.BlockSpec((TB, GLA_COLS), lambda b, j: (b * nj + j, 0)),
                  pl.BlockSpec((TB, SMALL_COLS), lambda b, j: (b * nj + j, 0)),
                  pl.BlockSpec((None, 1, H, DK, DV), lambda b, j: (s0_layer, b, 0, 0, 0)),
                  pl.BlockSpec((None,) + w2.shape[1:], lambda b, j: (layer, 0, 0)),
                  pl.BlockSpec(gb.shape, lambda b, j: (0, 0)),
                  pl.BlockSpec(ng.shape, lambda b, j: (0, 0))],
        out_specs=[pl.BlockSpec((TB, H * DV), lambda b, j: (b * nj + j, 0)),
                   pl.BlockSpec((1, H, DK, DV), lambda b, j: (b, 0, 0, 0))],
        out_shape=[jax.ShapeDtypeStruct((T, H * DV), F32),
                   jax.ShapeDtypeStruct((nseq, H, DK, DV), F32)],
        scratch_shapes=[pltpu.VMEM((H, DK, DV), F32)],
        compiler_params=_cparams(("parallel", "arbitrary")),
        name="gla",
    )(gla, small, s0, w2, gb, ng)


def _dsa_prompt_kernel(dq_ref, iq_ref, iwt_ref, small_ref, dk_ref, dvt_ref, y_ref,
                       key_scr, hi_scr, lo_scr, bias_scr, acc_scr, m_scr, j_scr, sa_scr, sb_scr, ma_scr, mb_scr,
                       *, L, topk, KC):
    QB = DSA_QBLOCK
    TPC = KC // QB
    G = DSA_GROUP
    i = pl.program_id(1)
    nch = (i + TPC) // TPC
    neg_inf = F32(-jnp.inf)
    row = lax.broadcasted_iota(I32, (QB, QB), 0)
    rowc = lax.broadcasted_iota(I32, (KC, QB), 0)
    qidx = i * QB + lax.broadcasted_iota(I32, (KC, QB), 1)

    iw_t = iwt_ref[...] * (IDX_HEADS ** -0.5 * IDX_DIM ** -0.5)
    iq_all = jnp.concatenate([iq_ref[:, h * IDX_DIM:(h + 1) * IDX_DIM] for h in range(IDX_HEADS)],
                             axis=0).astype(BF16)

    def score_chunk(c, carry):
        ks = pl.multiple_of(c * KC, KC)
        ik_c = small_ref[pl.ds(ks, KC), SM_IK:SM_IK + IDX_DIM].astype(BF16)
        d = _dot_nt(ik_c, iq_all)
        sc = jnp.maximum(d[:, 0:QB], 0.0) * iw_t[0:1, :]
        for h in range(1, IDX_HEADS):
            sc = sc + jnp.maximum(d[:, h * QB:(h + 1) * QB], 0.0) * iw_t[h:h + 1, :]
        sc = jnp.where(ks + rowc <= qidx, _pos_zero(sc), neg_inf)
        key = _float_key(sc)
        key_scr[pl.ds(ks, KC), :] = key
        hi_scr[pl.ds(ks, KC), :] = _key_hi16(key)
        lo_scr[pl.ds(ks, KC), :] = _key_lo16(key)
        return carry

    lax.fori_loop(0, nch, score_chunk, 0)

    def count(pred):
        def body(c, part):
            for t in range(TPC):
                ks = pl.multiple_of(c * KC + t * QB, QB)
                m = pred(key_scr[pl.ds(ks, QB), :], ks).astype(I32)
                part = part + m.reshape(QB // SUBLANES, SUBLANES, QB).sum(axis=0)
            return part
        part = lax.fori_loop(0, nch, body, jnp.zeros((SUBLANES, QB), I32))
        return part.sum(axis=0, keepdims=True)

    def count16_ge(plane, cand16):
        def body(c, part):
            ks = pl.multiple_of(c * KC, KC)
            ones = jnp.where(plane[pl.ds(ks, KC), :] >= cand16, I16(1), I16(0))
            ones = ones.reshape(KC // PACK16, PACK16, QB)
            return part + _tree_sum([ones[t] for t in range(KC // PACK16)])
        part = lax.fori_loop(0, nch, body, jnp.zeros((PACK16, QB), I16))
        return part.astype(I32).sum(axis=0, keepdims=True)

    def search16(plane, target):
        def bit(p, carry):
            t16, n_gt = carry
            cand = t16 + jnp.left_shift(I32(1), 15 - p)
            cnt = count16_ge(plane, cand.astype(I16))
            ok = cnt >= target
            return jnp.where(ok, cand, t16), jnp.where(ok, n_gt, cnt)
        return lax.fori_loop(0, 16, bit, (jnp.full((1, QB), I16_MIN, I32), jnp.zeros((1, QB), I32)))

    t_hi, gt_hi = search16(hi_scr, topk)
    t_hi16 = t_hi.astype(I16)

    def low_plane(c, carry):
        ks = pl.multiple_of(c * KC, KC)
        lo_scr[pl.ds(ks, KC), :] = jnp.where(hi_scr[pl.ds(ks, KC), :] == t_hi16, lo_scr[pl.ds(ks, KC), :],
                                              I16(I16_MIN))
        return carry

    lax.fori_loop(0, nch, low_plane, 0)
    t_lo, gt_lo = search16(lo_scr, topk - gt_hi)
    thr = t_hi * 65536 + (t_lo - I16_MIN)

    need = topk - gt_hi - gt_lo
    n_eq = count(lambda t, ks: t == thr)
    excess = (n_eq > need) & (thr > KEY_NEG_INF)
    j_scr[...] = jnp.full((1, QB), L, I32)
    nbits = int(L).bit_length()

    @pl.when(jnp.max(excess.astype(I32)) > 0)
    def _():
        def search_idx(p, lim):
            cand = lim + jnp.left_shift(I32(1), nbits - 1 - p)
            cnt = count(lambda t, ks: (t == thr) & (ks + row < cand))
            return jnp.where(cnt < need, cand, lim)
        lim = lax.fori_loop(0, nbits, search_idx, jnp.zeros((1, QB), I32))
        j_scr[...] = jnp.where(excess, lim, L)

    last_tie = j_scr[...]

    def bias_chunk(c, carry):
        for t in range(TPC):
            ks = pl.multiple_of(c * KC + t * QB, QB)
            tk = key_scr[pl.ds(ks, QB), :]
            sel = (tk > thr) | ((tk == thr) & (ks + row <= last_tie))
            bias_scr[pl.ds(ks, QB), :] = jnp.where(sel & (tk > KEY_NEG_INF), 0.0, neg_inf)
        return carry

    lax.fori_loop(0, nch, bias_chunk, 0)

    m_scr[...] = jnp.full(m_scr.shape, -1e30, F32)
    acc_scr[...] = jnp.zeros(acc_scr.shape, F32)
    zq = jnp.zeros((QB, HEAD_DIM), F32)
    q_rows = []
    for h in range(DSA_HEADS):
        qh = dq_ref[:, h * HEAD_DIM:(h + 1) * HEAD_DIM] * (HEAD_DIM ** -0.5 * LOG2E)
        parts = [qh if n == h // G else zq for n in range(DSA_KV_HEADS)]
        q_rows.append(jnp.concatenate(parts, axis=1))
    q_blk = jnp.concatenate(q_rows, axis=0).astype(BF16)
    ones_rows = jnp.ones((ONES_ROWS, KC), BF16)

    def scores(c, s_ref, cmax_ref):
        ks = pl.multiple_of(c * KC, KC)
        bias = bias_scr[pl.ds(ks, KC), :]
        k_c = dk_ref[pl.ds(ks, KC), :].astype(BF16)
        s = _dot_nt(k_c, q_blk) + jnp.concatenate([bias] * DSA_HEADS, axis=1)
        s_ref[...] = s
        cmax_ref[...] = jnp.max(s, axis=0, keepdims=True)

    def accumulate(c, s_ref, cmax_ref):
        vt_c = dvt_ref[c].astype(BF16)
        m_old = m_scr[...]
        m_new = jnp.maximum(m_old, cmax_ref[...])
        alpha = jnp.exp2(m_old - m_new)
        pb = jnp.exp2(s_ref[...] - m_new).astype(BF16)
        gw = G * QB
        for n in range(DSA_KV_HEADS):
            cols = slice(n * gw, (n + 1) * gw)
            vt1 = jnp.concatenate([vt_c[n * HEAD_DIM:(n + 1) * HEAD_DIM, :], ones_rows], axis=0)
            acc_scr[:, cols] = alpha[:, cols] * acc_scr[:, cols] + _dot(vt1, pb[:, cols])
        m_scr[...] = m_new

    def attend_pair(j, carry):
        c0 = 2 * j
        scores(jnp.minimum(c0 + 1, nch - 1), sb_scr, mb_scr)
        accumulate(c0, sa_scr, ma_scr)

        @pl.when(c0 + 1 < nch)
        def _():
            scores(jnp.minimum(c0 + 2, nch - 1), sa_scr, ma_scr)
            accumulate(c0 + 1, sb_scr, mb_scr)
        return carry

    scores(0, sa_scr, ma_scr)
    lax.fori_loop(0, (nch + 1) // 2, attend_pair, 0)

    o_t = acc_scr[0:HEAD_DIM, :] / acc_scr[HEAD_DIM:HEAD_DIM + 1, :]
    y_ref[...] = jnp.concatenate([o_t[:, h * QB:(h + 1) * QB].T for h in range(DSA_HEADS)], axis=1)


def _dsa_prompt(dq, iq, iw_t, small, dk, dv, *, nseq, seqlen):
    QB = DSA_QBLOCK
    T = dq.shape[0]
    nb = seqlen // QB
    topk = min(DSA_TOPK, seqlen // 4)
    kc = 4 * QB if seqlen % (4 * QB) == 0 else QB
    ncs = seqlen // kc
    nh = DSA_HEADS * QB
    dvt = dv.reshape(nseq * ncs, kc, DKV_COLS).transpose(0, 2, 1)
    return pl.pallas_call(
        functools.partial(_dsa_prompt_kernel, L=seqlen, topk=topk, KC=kc),
        grid=(nseq, nb),
        in_specs=[pl.BlockSpec((QB, DQ_COLS), lambda b, i: (b * nb + i, 0)),
                  pl.BlockSpec((QB, IQ_COLS), lambda b, i: (b * nb + i, 0)),
                  pl.BlockSpec((IDX_HEADS, QB), lambda b, i: (0, b * nb + i)),
                  pl.BlockSpec((seqlen, SMALL_COLS), lambda b, i: (b, 0)),
                  pl.BlockSpec((seqlen, DKV_COLS), lambda b, i: (b, 0)),
                  pl.BlockSpec((ncs, DKV_COLS, kc), lambda b, i: (b, 0, 0))],
        out_specs=pl.BlockSpec((QB, DQ_COLS), lambda b, i: (b * nb + i, 0)),
        out_shape=jax.ShapeDtypeStruct((T, DQ_COLS), F32),
        scratch_shapes=[pltpu.VMEM((seqlen, QB), I32),
                        pltpu.VMEM((seqlen, QB), I16),
                        pltpu.VMEM((seqlen, QB), I16),
                        pltpu.VMEM((seqlen, QB), F32),
                        pltpu.VMEM((HEAD_DIM + ONES_ROWS, nh), F32),
                        pltpu.VMEM((1, nh), F32),
                        pltpu.VMEM((1, QB), I32),
                        pltpu.VMEM((kc, nh), F32),
                        pltpu.VMEM((kc, nh), F32),
                        pltpu.VMEM((1, nh), F32),
                        pltpu.VMEM((1, nh), F32)],
        compiler_params=_cparams(("parallel", "arbitrary")),
        name="dsa_prompt",
    )(dq, iq, iw_t, small, dk, dvt)


def _swa_prompt_kernel(q_ref, kp_ref, kc_ref, vtp_ref, vtc_ref, sink_ref, y_ref):
    W = WINDOW
    G = SWA_GROUP
    i = pl.program_id(1)
    c = lax.broadcasted_iota(I32, (2 * W, W), 0)
    a = lax.broadcasted_iota(I32, (2 * W, W), 1)
    mask = (c > a) & (c <= a + W) & ((i > 0) | (c >= W))
    bias = jnp.where(mask, 0.0, F32(-jnp.inf))
    bias = jnp.concatenate([bias] * G, axis=1)
    kk = jnp.concatenate([kp_ref[...], kc_ref[...]], axis=0).astype(BF16)
    vt = jnp.concatenate([vtp_ref[...], vtc_ref[...]], axis=1).astype(BF16)
    ones_rows = jnp.ones((ONES_ROWS, 2 * W), BF16)
    outs = []
    for n in range(SWA_KV_HEADS):
        ns = slice(n * HEAD_DIM, (n + 1) * HEAD_DIM)
        qn = jnp.concatenate([q_ref[:, (n * G + g) * HEAD_DIM:(n * G + g + 1) * HEAD_DIM] for g in range(G)], axis=0)
        qn = (qn * (HEAD_DIM ** -0.5 * LOG2E)).astype(BF16)
        s = _dot_nt(kk[:, ns], qn) + bias
        sink = sink_ref[n] * LOG2E
        m = jnp.maximum(jnp.max(s, axis=0, keepdims=True), sink)
        p = jnp.exp2(s - m).astype(BF16)
        vt1 = jnp.concatenate([vt[ns, :], ones_rows], axis=0)
        pv = _dot(vt1, p)
        o_t = pv[0:HEAD_DIM] / (pv[HEAD_DIM:HEAD_DIM + 1] + jnp.exp2(sink - m))
        outs += [o_t[:, g * W:(g + 1) * W].T for g in range(G)]
    y_ref[...] = jnp.concatenate(outs, axis=1)


def _swa_prompt(q, k, v, sinks, *, nseq, seqlen):
    W = WINDOW
    T = q.shape[0]
    nb = seqlen // W
    nq, nk = SWA_HEADS * HEAD_DIM, SWA_KV_HEADS * HEAD_DIM
    cur = lambda b, i: (b * nb + i, 0)
    prev = lambda b, i: (b * nb + jnp.maximum(i - 1, 0), 0)
    cur_t = lambda b, i: (0, b * nb + i)
    prev_t = lambda b, i: (0, b * nb + jnp.maximum(i - 1, 0))
    vt = v.T
    return pl.pallas_call(
        _swa_prompt_kernel,
        grid=(nseq, nb),
        in_specs=[pl.BlockSpec((W, nq), cur),
                  pl.BlockSpec((W, nk), prev), pl.BlockSpec((W, nk), cur),
                  pl.BlockSpec((nk, W), prev_t), pl.BlockSpec((nk, W), cur_t),
                  pl.BlockSpec(sinks.shape, lambda b, i: (0, 0, 0))],
        out_specs=pl.BlockSpec((W, nq), cur),
        out_shape=jax.ShapeDtypeStruct((T, nq), F32),
        compiler_params=_cparams(("parallel", "parallel")),
        name="swa_prompt",
    )(q, k, k, vt, vt, sinks)


def _swa_sample_kernel(qb_ref, kn_ref, vn_ref, kt_ref, vt_ref, sink_ref, o_ref, kto_ref, vto_ref, *, Q):
    W = WINDOW
    R = SWA_HEADS * Q
    KD = SWA_KV_HEADS * HEAD_DIM
    neg_inf = F32(-jnp.inf)
    lane = lax.broadcasted_iota(I32, (KD, W), 1)

    def new_cols_t(x_ref):
        pad = jnp.concatenate([x_ref[...], jnp.zeros((W - Q, KD), F32)], axis=0)
        return pltpu.roll(pad.T, W - Q, axis=1)

    kt = kt_ref[0]
    vt = vt_ref[0]
    knt = new_cols_t(kn_ref)
    vnt = new_cols_t(vn_ref)
    kto_ref[0] = jnp.where(lane >= W - Q, knt, pltpu.roll(kt, W - Q, axis=1))
    vto_ref[0] = jnp.where(lane >= W - Q, vnt, pltpu.roll(vt, W - Q, axis=1))

    qb = (qb_ref[0] * (HEAD_DIM ** -0.5)).astype(BF16)
    qi = lax.broadcasted_iota(I32, (R, W), 0) % Q
    cc = lax.broadcasted_iota(I32, (R, W), 1)
    s_buf = jnp.where(cc > qi, _dot(qb, kt.astype(BF16)), neg_inf)
    s_new = jnp.where((cc >= W - Q) & (cc - (W - Q) <= qi), _dot(qb, knt.astype(BF16)), neg_inf)
    sink = sink_ref[...]
    m = jnp.maximum(jnp.maximum(jnp.max(s_buf, axis=-1, keepdims=True),
                                jnp.max(s_new, axis=-1, keepdims=True)), sink)
    p_buf = jnp.exp(s_buf - m)
    p_new = jnp.exp(s_new - m)
    den = jnp.sum(p_buf, axis=-1, keepdims=True) + jnp.sum(p_new, axis=-1, keepdims=True) + jnp.exp(sink - m)
    o = (_dot_nt(p_buf.astype(BF16), vt.astype(BF16)) + _dot_nt(p_new.astype(BF16), vnt.astype(BF16))) / den
    rpk = SWA_GROUP * Q
    o_ref[0] = jnp.concatenate(
        [o[n * rpk:(n + 1) * rpk, n * HEAD_DIM:(n + 1) * HEAD_DIM] for n in range(SWA_KV_HEADS)], axis=0)


def _swa_sample(qblk, k_new, v_new, kt, vt, layer, sink_col, *, nseq, Q):
    W = WINDOW
    R = SWA_HEADS * Q
    KD = SWA_KV_HEADS * HEAD_DIM
    return pl.pallas_call(
        functools.partial(_swa_sample_kernel, Q=Q),
        grid=(nseq,),
        in_specs=[pl.BlockSpec((1, R, KD), lambda b: (b, 0, 0)),
                  pl.BlockSpec((Q, KD), lambda b: (b, 0)),
                  pl.BlockSpec((Q, KD), lambda b: (b, 0)),
                  pl.BlockSpec((None, 1, KD, W), lambda b: (layer, b, 0, 0)),
                  pl.BlockSpec((None, 1, KD, W), lambda b: (layer, b, 0, 0)),
                  pl.BlockSpec((R, 1), lambda b: (0, 0))],
        out_specs=[pl.BlockSpec((1, R, HEAD_DIM), lambda b: (b, 0, 0)),
                   pl.BlockSpec((1, KD, W), lambda b: (b, 0, 0)),
                   pl.BlockSpec((1, KD, W), lambda b: (b, 0, 0))],
        out_shape=[jax.ShapeDtypeStruct((nseq, R, HEAD_DIM), F32),
                   jax.ShapeDtypeStruct((nseq, KD, W), F32),
                   jax.ShapeDtypeStruct((nseq, KD, W), F32)],
        compiler_params=_cparams(("parallel",)),
        name="swa_sample",
    )(qblk, k_new, v_new, kt, vt, sink_col)


def _fetch_pages(pt_ref, srcs, bufs, sem, *, layer, n_pages):
    P = PAGE_SIZE
    b = pl.program_id(0)
    nb = pl.num_programs(0)

    def copies(bb, slot):
        out = []
        for p in range(n_pages):
            page = pt_ref[bb, p]
            for j, (src, buf) in enumerate(zip(srcs, bufs)):
                out.append(pltpu.make_async_copy(src.at[layer, page], buf.at[slot, :, p * P:(p + 1) * P],
                                                 sem.at[slot, j]))
        return out

    slot = b % 2

    @pl.when(b == 0)
    def _():
        for cp in copies(0, 0):
            cp.start()

    @pl.when(b + 1 < nb)
    def _():
        for cp in copies(b + 1, 1 - slot):
            cp.start()

    for cp in copies(b, slot):
        cp.wait()
    return slot


def _dsa_sample_scores_kernel(pt_ref, ia_ref, iwc_ref, ikn_ref, ci_ref, key_ref, ibuf, sem, *, layer, n_pages, Q):
    P = PAGE_SIZE
    past = n_pages * P
    neg_inf = F32(-jnp.inf)
    slot = _fetch_pages(pt_ref, [ci_ref], [ibuf], sem, layer=layer, n_pages=n_pages)

    ia = ia_ref[0].astype(BF16)
    iwc = iwc_ref[0] * (IDX_HEADS ** -0.5 * IDX_DIM ** -0.5)

    def head_sum(d):
        d = jnp.maximum(d, 0.0) * iwc
        acc = d[0:Q]
        for h in range(1, IDX_HEADS):
            acc = acc + d[h * Q:(h + 1) * Q]
        return acc

    sc_past = head_sum(_dot(ia, ibuf[slot].astype(BF16)))
    ikn = jnp.concatenate([ikn_ref[...], jnp.zeros((P - Q, IDX_DIM), F32)], axis=0)
    sc_new = head_sum(_dot_nt(ia, ikn.astype(BF16)))
    qrow = lax.broadcasted_iota(I32, (Q, P), 0)
    lane = lax.broadcasted_iota(I32, (Q, P), 1)
    sc_new = jnp.where(lane <= qrow, sc_new, neg_inf)
    key_ref[0, :, 0:past] = _float_key(_pos_zero(sc_past))
    key_ref[0, :, past:past + P] = _float_key(_pos_zero(sc_new))


def _dsa_sample_select_kernel(key_ref, bias_ref, *, topk):
    GB, Q, N = key_ref.shape
    R = GB * Q
    keys = key_ref[...].reshape(R, N)
    idx = lax.broadcasted_iota(I32, (R, N), 1)

    def lane_count(mask):
        mi = mask.astype(I32)
        part = mi[:, 0:LANES]
        for t in range(1, N // LANES):
            part = part + mi[:, t * LANES:(t + 1) * LANES]
        return jnp.sum(part, axis=1, keepdims=True)

    def count16_ge(plane, cand):
        cand16 = jnp.broadcast_to(cand, (R, LANES)).astype(I16)
        part = _tree_sum([jnp.where(plane[:, t * LANES:(t + 1) * LANES] >= cand16, I16(1), I16(0))
                          for t in range(N // LANES)])
        return jnp.sum(part.astype(I32), axis=1, keepdims=True)

    def search16(plane, target):
        def bit(p, carry):
            t16, n_gt = carry
            cand = t16 + jnp.left_shift(I32(1), 15 - p)
            cnt = count16_ge(plane, cand)
            ok = cnt >= target
            return jnp.where(ok, cand, t16), jnp.where(ok, n_gt, cnt)
        return lax.fori_loop(0, 16, bit, (jnp.full((R, 1), I16_MIN, I32), jnp.zeros((R, 1), I32)))

    hi = _key_hi16(keys)
    t_hi, gt_hi = search16(hi, topk)
    lo = jnp.where(hi == jnp.broadcast_to(t_hi, (R, N)).astype(I16), _key_lo16(keys), I16(I16_MIN))
    t_lo, gt_lo = search16(lo, topk - gt_hi)
    thr = t_hi * 65536 + (t_lo - I16_MIN)

    need = topk - gt_hi - gt_lo
    n_eq = lane_count(keys == thr)
    excess = (n_eq > need) & (thr > KEY_NEG_INF)
    nbits = int(N).bit_length()

    def search_idx(p, lim):
        cand = lim + jnp.left_shift(I32(1), nbits - 1 - p)
        return jnp.where(lane_count((keys == thr) & (idx < cand)) < need, cand, lim)

    lim = lax.cond(jnp.max(excess.astype(I32)) > 0,
                   lambda: lax.fori_loop(0, nbits, search_idx, jnp.zeros((R, 1), I32)),
                   lambda: jnp.zeros((R, 1), I32))
    last_tie = jnp.where(excess, lim, N)
    sel = ((keys > thr) | ((keys == thr) & (idx <= last_tie))) & (keys > KEY_NEG_INF)
    bias_ref[...] = jnp.where(sel, 0.0, F32(-jnp.inf)).reshape(GB, Q, N)


def _dsa_sample_attend_kernel(pt_ref, bias_ref, qb_ref, kn_ref, vn_ref, ck_ref, cv_ref, o_ref,
                              kbuf, vbuf, sem, *, layer, n_pages, Q):
    P = PAGE_SIZE
    past = n_pages * P
    KD = DSA_KV_HEADS * HEAD_DIM
    slot = _fetch_pages(pt_ref, [ck_ref, cv_ref], [kbuf, vbuf], sem, layer=layer, n_pages=n_pages)
    bias_past = bias_ref[0, :, 0:past]
    bias_new = bias_ref[0, :, past:past + P]

    qb = (qb_ref[0] * (HEAD_DIM ** -0.5)).astype(BF16)
    kn = jnp.concatenate([kn_ref[...], jnp.zeros((P - Q, KD), F32)], axis=0).astype(BF16)
    vn = jnp.concatenate([vn_ref[...], jnp.zeros((P - Q, KD), F32)], axis=0).astype(BF16)
    s_past = _dot(qb, kbuf[slot].astype(BF16)) + jnp.concatenate([bias_past] * DSA_HEADS, axis=0)
    s_new = _dot_nt(qb, kn) + jnp.concatenate([bias_new] * DSA_HEADS, axis=0)
    m = jnp.maximum(jnp.max(s_past, axis=-1, keepdims=True), jnp.max(s_new, axis=-1, keepdims=True))
    p_past = jnp.exp(s_past - m)
    p_new = jnp.exp(s_new - m)
    den = jnp.sum(p_past, axis=-1, keepdims=True) + jnp.sum(p_new, axis=-1, keepdims=True)
    o = (_dot_nt(p_past.astype(BF16), vbuf[slot].astype(BF16)) + _dot(p_new.astype(BF16), vn)) / den
    rpk = DSA_GROUP * Q
    o_ref[0] = jnp.concatenate(
        [o[n * rpk:(n + 1) * rpk, n * HEAD_DIM:(n + 1) * HEAD_DIM] for n in range(DSA_KV_HEADS)], axis=0)


def _dsa_sample(page_table, ia, iwc, ik_new, qblk, k_new, v_new, ci_t, ck_t, cv_t, *, layer, nseq, Q):
    P = PAGE_SIZE
    n_pages = page_table.shape[1]
    past = n_pages * P
    KD = DSA_KV_HEADS * HEAD_DIM
    R = DSA_HEADS * Q
    RI = IDX_HEADS * Q
    topk = min(DSA_TOPK, (past + Q) // 4)
    N = past + P
    keys = pl.pallas_call(
        functools.partial(_dsa_sample_scores_kernel, layer=layer, n_pages=n_pages, Q=Q),
        grid_spec=pltpu.PrefetchScalarGridSpec(
            num_scalar_prefetch=1,
            grid=(nseq,),
            in_specs=[pl.BlockSpec((1, RI, IDX_DIM), lambda b, pt: (b, 0, 0)),
                      pl.BlockSpec((1, RI, 1), lambda b, pt: (b, 0, 0)),
                      pl.BlockSpec((Q, IDX_DIM), lambda b, pt: (b, 0)),
                      pl.BlockSpec(memory_space=pl.ANY)],
            out_specs=pl.BlockSpec((1, Q, N), lambda b, pt: (b, 0, 0)),
            scratch_shapes=[pltpu.VMEM((2, IDX_DIM, past), F32),
                            pltpu.SemaphoreType.DMA((2, 1))]),
        out_shape=jax.ShapeDtypeStruct((nseq, Q, N), I32),
        compiler_params=_cparams(("arbitrary",)),
        name="dsa_sample_scores",
    )(page_table, ia, iwc, ik_new, ci_t)

    gb = 8 if nseq % 8 == 0 else 1
    bias = pl.pallas_call(
        functools.partial(_dsa_sample_select_kernel, topk=topk),
        grid=(nseq // gb,),
        in_specs=[pl.BlockSpec((gb, Q, N), lambda g: (g, 0, 0))],
        out_specs=pl.BlockSpec((gb, Q, N), lambda g: (g, 0, 0)),
        out_shape=jax.ShapeDtypeStruct((nseq, Q, N), F32),
        compiler_params=_cparams(("parallel",)),
        name="dsa_sample_select",
    )(keys)

    return pl.pallas_call(
        functools.partial(_dsa_sample_attend_kernel, layer=layer, n_pages=n_pages, Q=Q),
        grid_spec=pltpu.PrefetchScalarGridSpec(
            num_scalar_prefetch=1,
            grid=(nseq,),
            in_specs=[pl.BlockSpec((1, Q, N), lambda b, pt: (b, 0, 0)),
                      pl.BlockSpec((1, R, KD), lambda b, pt: (b, 0, 0)),
                      pl.BlockSpec((Q, KD), lambda b, pt: (b, 0)),
                      pl.BlockSpec((Q, KD), lambda b, pt: (b, 0)),
                      pl.BlockSpec(memory_space=pl.ANY),
                      pl.BlockSpec(memory_space=pl.ANY)],
            out_specs=pl.BlockSpec((1, R, HEAD_DIM), lambda b, pt: (b, 0, 0)),
            scratch_shapes=[pltpu.VMEM((2, KD, past), F32),
                            pltpu.VMEM((2, KD, past), F32),
                            pltpu.SemaphoreType.DMA((2, 2))]),
        out_shape=jax.ShapeDtypeStruct((nseq, R, HEAD_DIM), F32),
        compiler_params=_cparams(("arbitrary",)),
        name="dsa_sample_attend",
    )(page_table, bias, qblk, k_new, v_new, ck_t, cv_t)


def _even_weight(w):
    offs = np.cumsum((0,) + EVEN_COLS)
    seg = lambda j: w[:, offs[j]:offs[j + 1]]
    gq, gk, gv, gr, gg, dq, dk, dv, iq, ik, iw = [seg(j) for j in range(len(EVEN_COLS))]
    pad = jnp.zeros((w.shape[0], SMALL_COLS - IDX_DIM - GLA_GATE_RANK - IDX_HEADS), w.dtype)
    return jnp.concatenate([gq, gk, gv, gr, dq, iq, dk, dv, ik, gg, iw, pad], axis=1).astype(BF16)


def _rope_tables(pos, reps):
    half = HEAD_DIM // 2
    inv = ROPE_THETA ** (-jnp.arange(half, dtype=F32) / half)
    ang = pos.astype(F32)[:, None] * inv[None, :]
    cos = jnp.cos(ang)
    sin = jnp.sin(ang)
    cos = jnp.tile(jnp.concatenate([cos, cos], axis=1), (reps, LANES // HEAD_DIM))
    sin = jnp.tile(jnp.concatenate([-sin, sin], axis=1), (reps, LANES // HEAD_DIM))
    return cos, sin


def _block_diag_queries(q, nseq, Q, n_kv, group):
    q5 = q.reshape(nseq, Q, n_kv, group, HEAD_DIM).transpose(0, 2, 3, 1, 4)
    eye = jnp.eye(n_kv, dtype=q.dtype)
    blk = q5[:, :, :, :, None, :] * eye[None, :, None, None, :, None]
    return blk.reshape(nseq, n_kv * group * Q, n_kv * HEAD_DIM)


def _rows_to_tokens(o, nseq, Q, heads):
    return o.reshape(nseq, heads, Q, HEAD_DIM).transpose(0, 2, 1, 3).reshape(nseq * Q, heads * HEAD_DIM)


def kernel(x_prompt, x_sample, cache_k, cache_v, cache_idx, state_gla, state_swa_k, state_swa_v, page_table,
           w_in_even, gla_gate_w2, gla_gate_b, gla_norm_g, w_out_even, w_in_odd, swa_sinks, w_out_odd,
           ffn_w_gu, ffn_w_down, ln_g, ln_b):
    B, L, D = x_prompt.shape
    Bd, Q, _ = x_sample.shape
    depth = ffn_w_gu.shape[0]
    d_ff = ffn_w_down.shape[1]
    n_even, n_pool = cache_k.shape[0], cache_k.shape[1]
    alpha = (2.0 * depth) ** 0.25
    tm = 512 if (B * L) % 512 == 0 and (Bd * Q) % 512 == 0 else 128
    ff_chunk = 256

    w_even = [_even_weight(w_in_even[i]) for i in range(n_even)]
    w_odd = w_in_odd.astype(BF16)
    wo_even = w_out_even.astype(BF16)
    wo_odd = w_out_odd.astype(BF16)
    w_gu = ffn_w_gu.astype(BF16)
    w_dn = ffn_w_down.astype(BF16)
    w2 = gla_gate_w2.astype(BF16)
    s0_prompt = jnp.zeros((1, B, GLA_HEADS, GLA_DK, GLA_DV), F32)

    ci_t = jnp.swapaxes(cache_idx, 2, 3)
    kd = DSA_KV_HEADS * HEAD_DIM
    ck_t = cache_k.transpose(0, 1, 3, 4, 2).reshape(n_even, n_pool, kd, PAGE_SIZE)
    cv_t = cache_v.transpose(0, 1, 3, 4, 2).reshape(n_even, n_pool, kd, PAGE_SIZE)
    skd = SWA_KV_HEADS * HEAD_DIM
    n_odd = state_swa_k.shape[0]
    swk_t = state_swa_k.transpose(0, 1, 3, 4, 2).reshape(n_odd, Bd, skd, WINDOW)
    swv_t = state_swa_v.transpose(0, 1, 3, 4, 2).reshape(n_odd, Bd, skd, WINDOW)

    cos_p, sin_p = _rope_tables(jnp.arange(L), 1)
    past_len = page_table.shape[1] * PAGE_SIZE
    cos_s, sin_s = _rope_tables(past_len + jnp.arange(Q), Bd)

    def finish_layer(l, x, lhs, w_out):
        h = _outproj_ln(x, lhs, w_out, l // 2, ln_g[l, 0][None], ln_b[l, 0][None], alpha, tm)
        return _ffn_ln(h, w_gu, w_dn, l, ln_g[l, 1][None], ln_b[l, 1][None], alpha, tm, ff_chunk)

    def even_common(i, x):
        return _project(x, w_even[i], EVEN_SPLITS, tm)

    x = x_prompt.reshape(B * L, D)
    ev_p, od_p = [], []
    for l in range(depth):
        i = l // 2
        if l % 2 == 0:
            gla, dq, iq, dk, dv, small = even_common(i, x)
            y_gla, s_fin = _gla(gla, small, s0_prompt, 0, w2, i, gla_gate_b[i][None], gla_norm_g[i][None],
                                nseq=B, seqlen=L, C=64, TB=256)
            iw_t = small[:, SM_IW:SM_IW + IDX_HEADS].T
            y_dsa = _dsa_prompt(dq, iq, iw_t, small, dk, dv, nseq=B, seqlen=L)
            ev_p.append((dk.reshape(B, L, DSA_KV_HEADS, HEAD_DIM), dv.reshape(B, L, DSA_KV_HEADS, HEAD_DIM),
                         small[:, SM_IK:SM_IK + IDX_DIM].reshape(B, L, IDX_DIM), s_fin))
            x = finish_layer(l, x, [y_gla, y_dsa], wo_even)
        else:
            q, k, v = _project_rope(x, w_odd, i, cos_p, sin_p, tm)
            sink_rows = jnp.repeat(swa_sinks[i], WINDOW).reshape(SWA_KV_HEADS, 1, SWA_GROUP * WINDOW)
            y = _swa_prompt(q, k, v, sink_rows, nseq=B, seqlen=L)
            k4 = k.reshape(B, L, SWA_KV_HEADS, HEAD_DIM)
            v4 = v.reshape(B, L, SWA_KV_HEADS, HEAD_DIM)
            od_p.append((k4[:, L - WINDOW:], v4[:, L - WINDOW:]))
            x = finish_layer(l, x, [y], wo_odd)
    y_prompt = x.reshape(B, L, D)

    x = x_sample.reshape(Bd * Q, D)
    ev_s, od_s = [], []
    for l in range(depth):
        i = l // 2
        if l % 2 == 0:
            gla, dq, iq, dk, dv, small = even_common(i, x)
            y_gla, s_fin = _gla(gla, small, state_gla, i, w2, i, gla_gate_b[i][None], gla_norm_g[i][None],
                                nseq=Bd, seqlen=Q, C=Q, TB=Q)
            ik_new = small[:, SM_IK:SM_IK + IDX_DIM]
            iw = small[:, SM_IW:SM_IW + IDX_HEADS]
            ia = iq.reshape(Bd, Q, IDX_HEADS, IDX_DIM).transpose(0, 2, 1, 3).reshape(Bd, IDX_HEADS * Q, IDX_DIM)
            iwc = iw.reshape(Bd, Q, IDX_HEADS).transpose(0, 2, 1).reshape(Bd, IDX_HEADS * Q, 1)
            qblk = _block_diag_queries(dq, Bd, Q, DSA_KV_HEADS, DSA_GROUP)
            o = _dsa_sample(page_table, ia, iwc, ik_new, qblk, dk, dv, ci_t, ck_t, cv_t, layer=i, nseq=Bd, Q=Q)
            y_dsa = _rows_to_tokens(o, Bd, Q, DSA_HEADS)
            ev_s.append((dk.reshape(Bd, Q, DSA_KV_HEADS, HEAD_DIM), dv.reshape(Bd, Q, DSA_KV_HEADS, HEAD_DIM),
                         ik_new.reshape(Bd, Q, IDX_DIM), s_fin))
            x = finish_layer(l, x, [y_gla, y_dsa], wo_even)
        else:
            q, k, v = _project_rope(x, w_odd, i, cos_s, sin_s, tm)
            qblk = _block_diag_queries(q, Bd, Q, SWA_KV_HEADS, SWA_GROUP)
            sink_col = jnp.repeat(swa_sinks[i], Q)[:, None]
            o, kt_new, vt_new = _swa_sample(qblk, k, v, swk_t, swv_t, i, sink_col, nseq=Bd, Q=Q)
            y = _rows_to_tokens(o, Bd, Q, SWA_HEADS)
            back = lambda t: t.reshape(Bd, SWA_KV_HEADS, HEAD_DIM, WINDOW).transpose(0, 3, 1, 2)
            od_s.append((back(kt_new), back(vt_new)))
            x = finish_layer(l, x, [y], wo_odd)
    y_sample = x.reshape(Bd, Q, D)

    stack = lambda states: [jnp.stack(z) for z in zip(*states)]
    k_p, v_p, idx_p, gla_p = stack(ev_p)
    swk_p, swv_p = stack(od_p)
    k_s, v_s, idx_s, gla_s = stack(ev_s)
    swk_s, swv_s = stack(od_s)
    return (y_prompt, y_sample, k_p, v_p, idx_p, gla_p, swk_p, swv_p, k_s, v_s, idx_s, gla_s, swk_s, swv_s)
```

```python
import functools

import jax
import jax.numpy as jnp
import numpy as np
from jax import lax
from jax.experimental import pallas as pl
from jax.experimental.pallas import tpu as pltpu

F32 = jnp.float32
BF16 = jnp.bfloat16
I32 = jnp.int32

GLA_HEADS = 4
GLA_DK = 64
GLA_DV = 128
GLA_GATE_RANK = 16
GLA_TAU = 16.0
DSA_HEADS = 8
DSA_KV_HEADS = 2
DSA_GROUP = DSA_HEADS // DSA_KV_HEADS
HEAD_DIM = 64
IDX_HEADS = 8
IDX_DIM = 64
DSA_TOPK = 256
DSA_QBLOCK = 128
SWA_HEADS = 16
SWA_KV_HEADS = 4
SWA_GROUP = SWA_HEADS // SWA_KV_HEADS
WINDOW = 128
ROPE_THETA = 150000.0
PAGE_SIZE = 128
LN_EPS = 1e-5
RMS_EPS = 1e-6

EVEN_COLS = (GLA_HEADS * GLA_DK, GLA_HEADS * GLA_DK, GLA_HEADS * GLA_DV, GLA_HEADS * GLA_DV, GLA_GATE_RANK,
             DSA_HEADS * HEAD_DIM, DSA_KV_HEADS * HEAD_DIM, DSA_KV_HEADS * HEAD_DIM,
             IDX_HEADS * IDX_DIM, IDX_DIM, IDX_HEADS)
GLA_COLS = 2 * GLA_HEADS * GLA_DK + 2 * GLA_HEADS * GLA_DV
DQ_COLS = DSA_HEADS * HEAD_DIM
IQ_COLS = IDX_HEADS * IDX_DIM
DKV_COLS = DSA_KV_HEADS * HEAD_DIM
SMALL_COLS = 128
SM_IK = 0
SM_GG = IDX_DIM
SM_IW = IDX_DIM + GLA_GATE_RANK
EVEN_SPLITS = (GLA_COLS, DQ_COLS, IQ_COLS, DKV_COLS, DKV_COLS, SMALL_COLS)

LANES = 128
SUBLANES = 8
VMEM_LIMIT_BYTES = 56 * 1024 * 1024

ONES_ROWS = 2 * SUBLANES
LOG2E = 1.4426950408889634
INT_MIN = np.int32(-(2 ** 31))
KEY_NEG_INF = np.int32(np.array(-np.inf, np.float32).view(np.int32) ^ np.int32(0x7FFFFFFF))
GLA_EXP_CLAMP = 80.0


def _cparams(sem):
    return pltpu.CompilerParams(dimension_semantics=sem, vmem_limit_bytes=VMEM_LIMIT_BYTES)


def _dot(a, b):
    return jnp.dot(a, b, preferred_element_type=F32)


def _dot_nt(a, b):
    return lax.dot_general(a, b, (((1,), (1,)), ((), ())), preferred_element_type=F32)


def _dot_tn(a, b):
    return lax.dot_general(a, b, (((0,), (0,)), ((), ())), preferred_element_type=F32)


def _layer_norm(v, g, b):
    mu = jnp.mean(v, axis=-1, keepdims=True)
    c = v - mu
    var = jnp.mean(c * c, axis=-1, keepdims=True)
    return c * lax.rsqrt(var + LN_EPS) * g + b


def _silu(v):
    return v * jax.nn.sigmoid(v)


def _key_float(key):
    bits = jnp.where(key < 0, key ^ np.int32(0x7FFFFFFF), key)
    return jnp.where(key < KEY_NEG_INF, F32(-jnp.inf), lax.bitcast_convert_type(bits, F32))


def _proj_kernel(x_ref, w_ref, *o_refs, splits):
    x = x_ref[...].astype(BF16)
    off = 0
    for o_ref, n in zip(o_refs, splits):
        o_ref[...] = _dot(x, w_ref[:, off:off + n])
        off += n


def _project(x, w, splits, tm):
    T, D = x.shape
    N = w.shape[1]
    return pl.pallas_call(
        functools.partial(_proj_kernel, splits=splits),
        grid=(T // tm,),
        in_specs=[pl.BlockSpec((tm, D), lambda i: (i, 0)),
                  pl.BlockSpec((D, N), lambda i: (0, 0))],
        out_specs=[pl.BlockSpec((tm, n), lambda i: (i, 0)) for n in splits],
        out_shape=[jax.ShapeDtypeStruct((T, n), F32) for n in splits],
        compiler_params=_cparams(("parallel",)),
        name="proj_even",
    )(x, w)


def _proj_rope_kernel(x_ref, w_ref, cos_ref, sin_ref, q_ref, k_ref, v_ref, *, nq, nk):
    x = x_ref[...].astype(BF16)
    cos = cos_ref[...]
    sin = sin_ref[...]
    half = HEAD_DIM // 2
    first = lax.broadcasted_iota(I32, cos.shape, 1) % HEAD_DIM < half

    def rope(u):
        swapped = jnp.where(first, pltpu.roll(u, LANES - half, axis=1), pltpu.roll(u, half, axis=1))
        return u * cos + swapped * sin

    for j in range(nq // LANES):
        q_ref[:, j * LANES:(j + 1) * LANES] = rope(_dot(x, w_ref[:, j * LANES:(j + 1) * LANES]))
    for j in range(nk // LANES):
        k_ref[:, j * LANES:(j + 1) * LANES] = rope(_dot(x, w_ref[:, nq + j * LANES:nq + (j + 1) * LANES]))
    v_ref[...] = _dot(x, w_ref[:, nq + nk:nq + 2 * nk])


def _project_rope(x, w, layer, cos, sin, tm):
    T, D = x.shape
    N = w.shape[2]
    nq, nk = SWA_HEADS * HEAD_DIM, SWA_KV_HEADS * HEAD_DIM
    ntab = cos.shape[0] // tm
    return pl.pallas_call(
        functools.partial(_proj_rope_kernel, nq=nq, nk=nk),
        grid=(T // tm,),
        in_specs=[pl.BlockSpec((tm, D), lambda i: (i, 0)),
                  pl.BlockSpec((None, D, N), lambda i: (layer, 0, 0)),
                  pl.BlockSpec((tm, LANES), lambda i: (i % ntab, 0)),
                  pl.BlockSpec((tm, LANES), lambda i: (i % ntab, 0))],
        out_specs=[pl.BlockSpec((tm, nq), lambda i: (i, 0)),
                   pl.BlockSpec((tm, nk), lambda i: (i, 0)),
                   pl.BlockSpec((tm, nk), lambda i: (i, 0))],
        out_shape=[jax.ShapeDtypeStruct((T, nq), F32),
                   jax.ShapeDtypeStruct((T, nk), F32),
                   jax.ShapeDtypeStruct((T, nk), F32)],
        compiler_params=_cparams(("parallel",)),
        name="proj_odd_rope",
    )(x, w, cos, sin)


def _layer_tail_kernel(*refs, n_lhs, alpha, dff, chunk):
    x_ref = refs[0]
    lhs = refs[1:1 + n_lhs]
    ws = refs[1 + n_lhs:1 + 2 * n_lhs]
    ln_ref, wgu_ref, wd_ref, o_ref = refs[1 + 2 * n_lhs:]
    acc = alpha * x_ref[...]
    for a_ref, w_ref in zip(lhs, ws):
        acc = acc + _dot(a_ref[...].astype(BF16), w_ref[...])
    h = _layer_norm(acc, ln_ref[0:1, :], ln_ref[1:2, :])
    hb = h.astype(BF16)
    acc = alpha * h
    for c in range(dff // chunk):
        gate = _dot(hb, wgu_ref[:, c * chunk:(c + 1) * chunk])
        up = _dot(hb, wgu_ref[:, dff + c * chunk:dff + (c + 1) * chunk])
        act = (_silu(gate) * up).astype(BF16)
        acc = acc + _dot(act, wd_ref[c * chunk:(c + 1) * chunk, :])
    o_ref[...] = _layer_norm(acc, ln_ref[2:3, :], ln_ref[3:4, :])


def _layer_tail(x, lhs, w_out, out_layer, ln, wgu, wd, layer, alpha, tm, chunk):
    T, D = x.shape
    n = len(lhs)
    dff = wd.shape[1]
    once = pl.Buffered(1)
    in_specs = [pl.BlockSpec((tm, D), lambda i: (i, 0))]
    in_specs += [pl.BlockSpec((tm, a.shape[1]), lambda i: (i, 0)) for a in lhs]
    in_specs += [pl.BlockSpec((None, a.shape[1], D), lambda i, r=r: (out_layer, r, 0), pipeline_mode=once)
                 for r, a in enumerate(lhs)]
    in_specs += [pl.BlockSpec(ln.shape, lambda i: (0, 0)),
                 pl.BlockSpec((None,) + wgu.shape[1:], lambda i: (layer, 0, 0), pipeline_mode=once),
                 pl.BlockSpec((None,) + wd.shape[1:], lambda i: (layer, 0, 0), pipeline_mode=once)]
    return pl.pallas_call(
        functools.partial(_layer_tail_kernel, n_lhs=n, alpha=alpha, dff=dff, chunk=chunk),
        grid=(T // tm,),
        in_specs=in_specs,
        out_specs=pl.BlockSpec((tm, D), lambda i: (i, 0)),
        out_shape=jax.ShapeDtypeStruct((T, D), F32),
        compiler_params=_cparams(("parallel",)),
        name="layer_tail",
    )(x, *lhs, *([w_out] * n), ln, wgu, wd)


def _gla_kernel(gla_ref, small_ref, s0_ref, w2_ref, gb_ref, ng_ref, y_ref, sout_ref, s_scr, *, C, TB, NS):
    H, DK, DV = GLA_HEADS, GLA_DK, GLA_DV
    j = pl.program_id(1)

    @pl.when(j == 0)
    def _():
        s_scr[...] = s0_ref[...]

    rowc = lax.broadcasted_iota(I32, (C, C), 0)
    colc = lax.broadcasted_iota(I32, (C, C), 1)
    tril = rowc >= colc
    tril_b = tril.astype(BF16)
    eye_dk = lax.broadcasted_iota(I32, (DK, DK), 0) == lax.broadcasted_iota(I32, (DK, DK), 1)
    w2 = w2_ref[...]
    gb = gb_ref[...]
    ng = ng_ref[...]
    mid = C // 2 - 1

    for sq, c in [(sq, c) for sq in range(NS) for c in range(TB // C)]:
        rows = slice(sq * TB + c * C, sq * TB + (c + 1) * C)
        gg = small_ref[rows, SM_GG:SM_GG + GLA_GATE_RANK].astype(BF16)
        z = _dot(gg, w2) + gb
        glog = -(jnp.maximum(-z, 0.0) + jnp.log1p(jnp.exp(-jnp.abs(z)))) * (1.0 / GLA_TAU)
        g1 = glog.astype(BF16)
        r1 = glog - g1.astype(F32)
        g2 = r1.astype(BF16)
        g3 = (r1 - g2.astype(F32)).astype(BF16)
        b = _dot(tril_b, g1) + _dot(tril_b, g2) + _dot(tril_b, g3)
        for h in range(H):
            qh = gla_ref[rows, h * DK:(h + 1) * DK] * (DK ** -0.5)
            kh = gla_ref[rows, H * DK + h * DK:H * DK + (h + 1) * DK]
            vh = gla_ref[rows, 2 * H * DK + h * DV:2 * H * DK + (h + 1) * DV].astype(BF16)
            rh = gla_ref[rows, 2 * H * DK + H * DV + h * DV:2 * H * DK + H * DV + (h + 1) * DV]
            bh = b[:, h * DK:(h + 1) * DK]
            bl = bh[C - 1:C, :]
            rr = bh[mid:mid + 1, :]
            qt = (qh * jnp.exp(jnp.clip(bh - rr, -GLA_EXP_CLAMP, GLA_EXP_CLAMP))).astype(BF16)
            kt = (kh * jnp.exp(jnp.clip(rr - bh, -GLA_EXP_CLAMP, GLA_EXP_CLAMP))).astype(BF16)
            a = jnp.where(tril, _dot_nt(qt, kt), 0.0)
            s_prev = s_scr[sq, h]
            o = _dot(a.astype(BF16), vh) + _dot((qh * jnp.exp(bh)).astype(BF16), s_prev.astype(BF16))
            kin = (kh * jnp.exp(bl - bh)).astype(BF16)
            kv = _dot_tn(kin, vh)
            dcol = jnp.sum(jnp.where(eye_dk, jnp.broadcast_to(bl, (DK, DK)), 0.0), axis=1, keepdims=True)
            s_scr[sq, h] = jnp.exp(dcol) * s_prev + kv
            ms = jnp.mean(o * o, axis=-1, keepdims=True)
            y_ref[rows, h * DV:(h + 1) * DV] = o * lax.rsqrt(ms + RMS_EPS) * ng * _silu(rh)

    @pl.when(j == pl.num_programs(1) - 1)
    def _():
        sout_ref[...] = s_scr[...]


def _gla(gla, small, s0, s0_layer, w2, layer, gb, ng, *, nseq, seqlen, C, TB, NS=1):
    H, DK, DV = GLA_HEADS, GLA_DK, GLA_DV
    T = gla.shape[0]
    nj = seqlen // TB
    assert NS == 1 or nj == 1
    rows = NS * TB
    return pl.pallas_call(
        functools.partial(_gla_kernel, C=C, TB=TB, NS=NS),
        grid=(nseq // NS, nj),
        in_specs=[pl.BlockSpec((rows, GLA_COLS), lambda b, j: (b * nj + j, 0)),
                  pl.BlockSpec((rows, SMALL_COLS), lambda b, j: (b * nj + j, 0)),
                  pl.BlockSpec((None, NS, H, DK, DV), lambda b, j: (s0_layer, b, 0, 0, 0)),
                  pl.BlockSpec((None,) + w2.shape[1:], lambda b, j: (layer, 0, 0)),
                  pl.BlockSpec(gb.shape, lambda b, j: (0, 0)),
                  pl.BlockSpec(ng.shape, lambda b, j: (0, 0))],
        out_specs=[pl.BlockSpec((rows, H * DV), lambda b, j: (b * nj + j, 0)),
                   pl.BlockSpec((NS, H, DK, DV), lambda b, j: (b, 0, 0, 0))],
        out_shape=[jax.ShapeDtypeStruct((T, H * DV), F32),
                   jax.ShapeDtypeStruct((nseq, H, DK, DV), F32)],
        scratch_shapes=[pltpu.VMEM((NS, H, DK, DV), F32)],
        compiler_params=_cparams(("parallel", "arbitrary")),
        name="gla",
    )(gla, small, s0, w2, gb, ng)


def _dsa_prompt_kernel(dq_ref, iq_ref, iwt_ref, small_ref, dk_ref, dvt_ref, y_ref,
                       sc_scr, bias_scr, acc_scr, m_scr, j_scr, *, L, topk, KC):
    QB = DSA_QBLOCK
    TPC = KC // QB
    G = DSA_GROUP
    i = pl.program_id(1)
    nch = (i + TPC) // TPC
    neg_inf = F32(-jnp.inf)
    row = lax.broadcasted_iota(I32, (QB, QB), 0)
    rowc = lax.broadcasted_iota(I32, (KC, QB), 0)
    qidx = i * QB + lax.broadcasted_iota(I32, (KC, QB), 1)

    iw_t = iwt_ref[...] * (IDX_HEADS ** -0.5 * IDX_DIM ** -0.5)
    iq_all = jnp.concatenate([iq_ref[:, h * IDX_DIM:(h + 1) * IDX_DIM] for h in range(IDX_HEADS)],
                             axis=0).astype(BF16)

    def score_chunk(c, carry):
        ks = pl.multiple_of(c * KC, KC)
        ik_c = small_ref[pl.ds(ks, KC), SM_IK:SM_IK + IDX_DIM].astype(BF16)
        d = _dot_nt(ik_c, iq_all)
        sc = jnp.maximum(d[:, 0:QB], 0.0) * iw_t[0:1, :]
        for h in range(1, IDX_HEADS):
            sc = sc + jnp.maximum(d[:, h * QB:(h + 1) * QB], 0.0) * iw_t[h:h + 1, :]
        sc_scr[pl.ds(ks, KC), :] = jnp.where(ks + rowc <= qidx, sc, neg_inf)
        return carry

    lax.fori_loop(0, nch, score_chunk, 0)

    def count(pred):
        def body(c, part):
            for t in range(TPC):
                ks = pl.multiple_of(c * KC + t * QB, QB)
                m = pred(sc_scr[pl.ds(ks, QB), :], ks).astype(I32)
                part = part + m.reshape(QB // SUBLANES, SUBLANES, QB).sum(axis=0)
            return part
        part = lax.fori_loop(0, nch, body, jnp.zeros((SUBLANES, QB), I32))
        return part.sum(axis=0, keepdims=True)

    def search_bit(p, tkey):
        cand_key = tkey + jnp.left_shift(I32(1), 31 - p)
        cand = _key_float(cand_key)
        cnt = count(lambda s, ks: s >= cand)
        return jnp.where(cnt >= topk, cand_key, tkey)

    thr_key = lax.fori_loop(0, 32, search_bit, jnp.full((1, QB), INT_MIN, I32))
    thr = _key_float(thr_key)

    need = topk - count(lambda s, ks: s > thr)
    n_eq = count(lambda s, ks: s == thr)
    excess = (n_eq > need) & (thr > neg_inf)
    j_scr[...] = jnp.full((1, QB), L, I32)
    nbits = int(L).bit_length()

    @pl.when(jnp.max(excess.astype(I32)) > 0)
    def _():
        def search_idx(p, lim):
            cand = lim + jnp.left_shift(I32(1), nbits - 1 - p)
            cnt = count(lambda t, ks: (t == thr) & (ks + row < cand))
            return jnp.where(cnt < need, cand, lim)
        lim = lax.fori_loop(0, nbits, search_idx, jnp.zeros((1, QB), I32))
        j_scr[...] = jnp.where(excess, lim, L)

    last_tie = j_scr[...]

    def bias_chunk(c, carry):
        for t in range(TPC):
            ks = pl.multiple_of(c * KC + t * QB, QB)
            s = sc_scr[pl.ds(ks, QB), :]
            sel = (s > thr) | ((s == thr) & (ks + row <= last_tie))
            bias_scr[pl.ds(ks, QB), :] = jnp.where(sel & (s > neg_inf), 0.0, neg_inf)
        return carry

    lax.fori_loop(0, nch, bias_chunk, 0)

    m_scr[...] = jnp.full(m_scr.shape, -1e30, F32)
    acc_scr[...] = jnp.zeros(acc_scr.shape, F32)
    zq = jnp.zeros((QB, HEAD_DIM), F32)
    q_rows = []
    for h in range(DSA_HEADS):
        qh = dq_ref[:, h * HEAD_DIM:(h + 1) * HEAD_DIM] * (HEAD_DIM ** -0.5 * LOG2E)
        parts = [qh if n == h // G else zq for n in range(DSA_KV_HEADS)]
        q_rows.append(jnp.concatenate(parts, axis=1))
    q_blk = jnp.concatenate(q_rows, axis=0).astype(BF16)
    ones_rows = jnp.ones((ONES_ROWS, KC), BF16)

    def attend_chunk(c, carry):
        ks = pl.multiple_of(c * KC, KC)
        bias = bias_scr[pl.ds(ks, KC), :]
        k_c = dk_ref[pl.ds(ks, KC), :].astype(BF16)
        vt_c = dvt_ref[c].astype(BF16)
        s = _dot_nt(k_c, q_blk) + jnp.concatenate([bias] * DSA_HEADS, axis=1)
        m_old = m_scr[...]
        m_new = jnp.maximum(m_old, jnp.max(s, axis=0, keepdims=True))
        alpha = jnp.exp2(m_old - m_new)
        pb = jnp.exp2(s - m_new).astype(BF16)
        gw = G * QB
        for n in range(DSA_KV_HEADS):
            cols = slice(n * gw, (n + 1) * gw)
            vt1 = jnp.concatenate([vt_c[n * HEAD_DIM:(n + 1) * HEAD_DIM, :], ones_rows], axis=0)
            acc_scr[:, cols] = alpha[:, cols] * acc_scr[:, cols] + _dot(vt1, pb[:, cols])
        m_scr[...] = m_new
        return carry

    lax.fori_loop(0, nch, attend_chunk, 0)

    o_t = acc_scr[0:HEAD_DIM, :] / acc_scr[HEAD_DIM:HEAD_DIM + 1, :]
    y_ref[...] = jnp.concatenate([o_t[:, h * QB:(h + 1) * QB].T for h in range(DSA_HEADS)], axis=1)


def _dsa_prompt(dq, iq, iw_t, small, dk, dv, *, nseq, seqlen):
    QB = DSA_QBLOCK
    T = dq.shape[0]
    nb = seqlen // QB
    topk = min(DSA_TOPK, seqlen // 4)
    kc = 4 * QB if seqlen % (4 * QB) == 0 else QB
    ncs = seqlen // kc
    nh = DSA_HEADS * QB
    dvt = dv.reshape(nseq * ncs, kc, DKV_COLS).transpose(0, 2, 1)
    return pl.pallas_call(
        functools.partial(_dsa_prompt_kernel, L=seqlen, topk=topk, KC=kc),
        grid=(nseq, nb),
        in_specs=[pl.BlockSpec((QB, DQ_COLS), lambda b, i: (b * nb + i, 0)),
                  pl.BlockSpec((QB, IQ_COLS), lambda b, i: (b * nb + i, 0)),
                  pl.BlockSpec((IDX_HEADS, QB), lambda b, i: (0, b * nb + i)),
                  pl.BlockSpec((seqlen, SMALL_COLS), lambda b, i: (b, 0)),
                  pl.BlockSpec((seqlen, DKV_COLS), lambda b, i: (b, 0)),
                  pl.BlockSpec((ncs, DKV_COLS, kc), lambda b, i: (b, 0, 0))],
        out_specs=pl.BlockSpec((QB, DQ_COLS), lambda b, i: (b * nb + i, 0)),
        out_shape=jax.ShapeDtypeStruct((T, DQ_COLS), F32),
        scratch_shapes=[pltpu.VMEM((seqlen, QB), F32),
                        pltpu.VMEM((seqlen, QB), F32),
                        pltpu.VMEM((HEAD_DIM + ONES_ROWS, nh), F32),
                        pltpu.VMEM((1, nh), F32),
                        pltpu.VMEM((1, QB), I32)],
        compiler_params=_cparams(("parallel", "arbitrary")),
        name="dsa_prompt",
    )(dq, iq, iw_t, small, dk, dvt)


def _swa_prompt_kernel(q_ref, kp_ref, kc_ref, vtp_ref, vtc_ref, sink_ref, y_ref):
    W = WINDOW
    G = SWA_GROUP
    i = pl.program_id(1)
    c = lax.broadcasted_iota(I32, (2 * W, W), 0)
    a = lax.broadcasted_iota(I32, (2 * W, W), 1)
    mask = (c > a) & (c <= a + W) & ((i > 0) | (c >= W))
    bias = jnp.where(mask, 0.0, F32(-jnp.inf))
    bias = jnp.concatenate([bias] * G, axis=1)
    kk = jnp.concatenate([kp_ref[...], kc_ref[...]], axis=0).astype(BF16)
    vt = jnp.concatenate([vtp_ref[...], vtc_ref[...]], axis=1).astype(BF16)
    ones_rows = jnp.ones((ONES_ROWS, 2 * W), BF16)
    outs = []
    for n in range(SWA_KV_HEADS):
        ns = slice(n * HEAD_DIM, (n + 1) * HEAD_DIM)
        qn = jnp.concatenate([q_ref[:, (n * G + g) * HEAD_DIM:(n * G + g + 1) * HEAD_DIM] for g in range(G)], axis=0)
        qn = (qn * (HEAD_DIM ** -0.5 * LOG2E)).astype(BF16)
        s = _dot_nt(kk[:, ns], qn) + bias
        sink = sink_ref[n] * LOG2E
        m = jnp.maximum(jnp.max(s, axis=0, keepdims=True), sink)
        p = jnp.exp2(s - m).astype(BF16)
        vt1 = jnp.concatenate([vt[ns, :], ones_rows], axis=0)
        pv = _dot(vt1, p)
        o_t = pv[0:HEAD_DIM] / (pv[HEAD_DIM:HEAD_DIM + 1] + jnp.exp2(sink - m))
        outs += [o_t[:, g * W:(g + 1) * W].T for g in range(G)]
    y_ref[...] = jnp.concatenate(outs, axis=1)


def _swa_prompt(q, k, v, sinks, *, nseq, seqlen):
    W = WINDOW
    T = q.shape[0]
    nb = seqlen // W
    nq, nk = SWA_HEADS * HEAD_DIM, SWA_KV_HEADS * HEAD_DIM
    cur = lambda b, i: (b * nb + i, 0)
    prev = lambda b, i: (b * nb + jnp.maximum(i - 1, 0), 0)
    cur_t = lambda b, i: (0, b * nb + i)
    prev_t = lambda b, i: (0, b * nb + jnp.maximum(i - 1, 0))
    vt = v.T
    return pl.pallas_call(
        _swa_prompt_kernel,
        grid=(nseq, nb),
        in_specs=[pl.BlockSpec((W, nq), cur),
                  pl.BlockSpec((W, nk), prev), pl.BlockSpec((W, nk), cur),
                  pl.BlockSpec((nk, W), prev_t), pl.BlockSpec((nk, W), cur_t),
                  pl.BlockSpec(sinks.shape, lambda b, i: (0, 0, 0))],
        out_specs=pl.BlockSpec((W, nq), cur),
        out_shape=jax.ShapeDtypeStruct((T, nq), F32),
        compiler_params=_cparams(("parallel", "parallel")),
        name="swa_prompt",
    )(q, k, k, vt, vt, sinks)


def _swa_sample_kernel(qb_ref, kn_ref, vn_ref, kt_ref, vt_ref, sink_ref, o_ref, kto_ref, vto_ref, *, Q, NS):
    W = WINDOW
    R = SWA_HEADS * Q
    KD = SWA_KV_HEADS * HEAD_DIM
    neg_inf = F32(-jnp.inf)
    lane = lax.broadcasted_iota(I32, (KD, W), 1)
    qi = lax.broadcasted_iota(I32, (R, W), 0) % Q
    cc = lax.broadcasted_iota(I32, (R, W), 1)
    sink = sink_ref[...]
    rpk = SWA_GROUP * Q

    def new_cols_t(x):
        pad = jnp.concatenate([x, jnp.zeros((W - Q, KD), F32)], axis=0)
        return pltpu.roll(pad.T, W - Q, axis=1)

    for sq in range(NS):
        kt = kt_ref[sq]
        vt = vt_ref[sq]
        knt = new_cols_t(kn_ref[sq * Q:(sq + 1) * Q, :])
        vnt = new_cols_t(vn_ref[sq * Q:(sq + 1) * Q, :])
        kto_ref[sq] = jnp.where(lane >= W - Q, knt, pltpu.roll(kt, W - Q, axis=1))
        vto_ref[sq] = jnp.where(lane >= W - Q, vnt, pltpu.roll(vt, W - Q, axis=1))

        qb = (qb_ref[sq] * (HEAD_DIM ** -0.5)).astype(BF16)
        s_buf = jnp.where(cc > qi, _dot(qb, kt.astype(BF16)), neg_inf)
        s_new = jnp.where((cc >= W - Q) & (cc - (W - Q) <= qi), _dot(qb, knt.astype(BF16)), neg_inf)
        m = jnp.maximum(jnp.maximum(jnp.max(s_buf, axis=-1, keepdims=True),
                                    jnp.max(s_new, axis=-1, keepdims=True)), sink)
        p_buf = jnp.exp(s_buf - m)
        p_new = jnp.exp(s_new - m)
        den = jnp.sum(p_buf, axis=-1, keepdims=True) + jnp.sum(p_new, axis=-1, keepdims=True) + jnp.exp(sink - m)
        o = (_dot_nt(p_buf.astype(BF16), vt.astype(BF16)) + _dot_nt(p_new.astype(BF16), vnt.astype(BF16))) / den
        o_ref[sq] = jnp.concatenate(
            [o[n * rpk:(n + 1) * rpk, n * HEAD_DIM:(n + 1) * HEAD_DIM] for n in range(SWA_KV_HEADS)], axis=0)


def _swa_sample(qblk, k_new, v_new, kt, vt, layer, sink_col, *, nseq, Q, NS=1):
    W = WINDOW
    R = SWA_HEADS * Q
    KD = SWA_KV_HEADS * HEAD_DIM
    return pl.pallas_call(
        functools.partial(_swa_sample_kernel, Q=Q, NS=NS),
        grid=(nseq // NS,),
        in_specs=[pl.BlockSpec((NS, R, KD), lambda b: (b, 0, 0)),
                  pl.BlockSpec((NS * Q, KD), lambda b: (b, 0)),
                  pl.BlockSpec((NS * Q, KD), lambda b: (b, 0)),
                  pl.BlockSpec((None, NS, KD, W), lambda b: (layer, b, 0, 0)),
                  pl.BlockSpec((None, NS, KD, W), lambda b: (layer, b, 0, 0)),
                  pl.BlockSpec((R, 1), lambda b: (0, 0))],
        out_specs=[pl.BlockSpec((NS, R, HEAD_DIM), lambda b: (b, 0, 0)),
                   pl.BlockSpec((NS, KD, W), lambda b: (b, 0, 0)),
                   pl.BlockSpec((NS, KD, W), lambda b: (b, 0, 0))],
        out_shape=[jax.ShapeDtypeStruct((nseq, R, HEAD_DIM), F32),
                   jax.ShapeDtypeStruct((nseq, KD, W), F32),
                   jax.ShapeDtypeStruct((nseq, KD, W), F32)],
        compiler_params=_cparams(("parallel",)),
        name="swa_sample",
    )(qblk, k_new, v_new, kt, vt, sink_col)


def _fetch_pages(pt_ref, srcs, bufs, sem, *, layer, n_pages):
    P = PAGE_SIZE
    b = pl.program_id(0)
    nb = pl.num_programs(0)

    def copies(bb, slot):
        out = []
        for p in range(n_pages):
            page = pt_ref[bb, p]
            for j, (src, buf) in enumerate(zip(srcs, bufs)):
                out.append(pltpu.make_async_copy(src.at[layer, page], buf.at[slot, :, p * P:(p + 1) * P],
                                                 sem.at[slot, j]))
        return out

    slot = b % 2

    @pl.when(b == 0)
    def _():
        for cp in copies(0, 0):
            cp.start()

    @pl.when(b + 1 < nb)
    def _():
        for cp in copies(b + 1, 1 - slot):
            cp.start()

    for cp in copies(b, slot):
        cp.wait()
    return slot


def _dsa_sample_scores_kernel(pt_ref, ia_ref, iwc_ref, ikn_ref, ci_ref, key_ref, ibuf, sem, *, layer, n_pages, Q):
    P = PAGE_SIZE
    past = n_pages * P
    neg_inf = F32(-jnp.inf)
    slot = _fetch_pages(pt_ref, [ci_ref], [ibuf], sem, layer=layer, n_pages=n_pages)

    ia = ia_ref[0].astype(BF16)
    iwc = iwc_ref[0] * (IDX_HEADS ** -0.5 * IDX_DIM ** -0.5)

    def head_sum(d):
        d = jnp.maximum(d, 0.0) * iwc
        acc = d[0:Q]
        for h in range(1, IDX_HEADS):
            acc = acc + d[h * Q:(h + 1) * Q]
        return acc

    sc_past = head_sum(_dot(ia, ibuf[slot].astype(BF16)))
    ikn = jnp.concatenate([ikn_ref[...], jnp.zeros((P - Q, IDX_DIM), F32)], axis=0)
    sc_new = head_sum(_dot_nt(ia, ikn.astype(BF16)))
    qrow = lax.broadcasted_iota(I32, (Q, P), 0)
    lane = lax.broadcasted_iota(I32, (Q, P), 1)
    sc_new = jnp.where(lane <= qrow, sc_new, neg_inf)
    key_ref[0, :, 0:past] = sc_past
    key_ref[0, :, past:past + P] = sc_new


def _dsa_sample_select_kernel(key_ref, bias_ref, *, topk):
    GB, Q, N = key_ref.shape
    R = GB * Q
    keys = key_ref[...].reshape(R, N)
    idx = lax.broadcasted_iota(I32, (R, N), 1)

    def lane_count(mask):
        mi = mask.astype(I32)
        part = mi[:, 0:LANES]
        for t in range(1, N // LANES):
            part = part + mi[:, t * LANES:(t + 1) * LANES]
        return jnp.sum(part, axis=1, keepdims=True)

    def search_bit(p, tkey):
        cand_key = tkey + jnp.left_shift(I32(1), 31 - p)
        return jnp.where(lane_count(keys >= _key_float(cand_key)) >= topk, cand_key, tkey)

    thr = _key_float(lax.fori_loop(0, 32, search_bit, jnp.full((R, 1), INT_MIN, I32)))

    neg_inf = F32(-jnp.inf)
    need = topk - lane_count(keys > thr)
    n_eq = lane_count(keys == thr)
    excess = (n_eq > need) & (thr > neg_inf)
    nbits = int(N).bit_length()

    def search_idx(p, lim):
        cand = lim + jnp.left_shift(I32(1), nbits - 1 - p)
        return jnp.where(lane_count((keys == thr) & (idx < cand)) < need, cand, lim)

    lim = lax.cond(jnp.max(excess.astype(I32)) > 0,
                   lambda: lax.fori_loop(0, nbits, search_idx, jnp.zeros((R, 1), I32)),
                   lambda: jnp.zeros((R, 1), I32))
    last_tie = jnp.where(excess, lim, N)
    sel = ((keys > thr) | ((keys == thr) & (idx <= last_tie))) & (keys > neg_inf)
    bias_ref[...] = jnp.where(sel, 0.0, neg_inf).reshape(GB, Q, N)


def _dsa_sample_attend_kernel(pt_ref, bias_ref, qb_ref, kn_ref, vn_ref, ck_ref, cv_ref, o_ref,
                              kbuf, vbuf, sem, *, layer, n_pages, Q):
    P = PAGE_SIZE
    past = n_pages * P
    KD = DSA_KV_HEADS * HEAD_DIM
    slot = _fetch_pages(pt_ref, [ck_ref, cv_ref], [kbuf, vbuf], sem, layer=layer, n_pages=n_pages)
    bias_past = bias_ref[0, :, 0:past]
    bias_new = bias_ref[0, :, past:past + P]

    qb = (qb_ref[0] * (HEAD_DIM ** -0.5)).astype(BF16)
    kn = jnp.concatenate([kn_ref[...], jnp.zeros((P - Q, KD), F32)], axis=0).astype(BF16)
    vn = jnp.concatenate([vn_ref[...], jnp.zeros((P - Q, KD), F32)], axis=0).astype(BF16)
    s_past = _dot(qb, kbuf[slot].astype(BF16)) + jnp.concatenate([bias_past] * DSA_HEADS, axis=0)
    s_new = _dot_nt(qb, kn) + jnp.concatenate([bias_new] * DSA_HEADS, axis=0)
    m = jnp.maximum(jnp.max(s_past, axis=-1, keepdims=True), jnp.max(s_new, axis=-1, keepdims=True))
    p_past = jnp.exp(s_past - m)
    p_new = jnp.exp(s_new - m)
    den = jnp.sum(p_past, axis=-1, keepdims=True) + jnp.sum(p_new, axis=-1, keepdims=True)
    o = (_dot_nt(p_past.astype(BF16), vbuf[slot].astype(BF16)) + _dot(p_new.astype(BF16), vn)) / den
    rpk = DSA_GROUP * Q
    o_ref[0] = jnp.concatenate(
        [o[n * rpk:(n + 1) * rpk, n * HEAD_DIM:(n + 1) * HEAD_DIM] for n in range(DSA_KV_HEADS)], axis=0)


def _dsa_sample(page_table, ia, iwc, ik_new, qblk, k_new, v_new, ci_t, ck_t, cv_t, *, layer, nseq, Q):
    P = PAGE_SIZE
    n_pages = page_table.shape[1]
    past = n_pages * P
    KD = DSA_KV_HEADS * HEAD_DIM
    R = DSA_HEADS * Q
    RI = IDX_HEADS * Q
    topk = min(DSA_TOPK, (past + Q) // 4)
    N = past + P
    keys = pl.pallas_call(
        functools.partial(_dsa_sample_scores_kernel, layer=layer, n_pages=n_pages, Q=Q),
        grid_spec=pltpu.PrefetchScalarGridSpec(
            num_scalar_prefetch=1,
            grid=(nseq,),
            in_specs=[pl.BlockSpec((1, RI, IDX_DIM), lambda b, pt: (b, 0, 0)),
                      pl.BlockSpec((1, RI, 1), lambda b, pt: (b, 0, 0)),
                      pl.BlockSpec((Q, IDX_DIM), lambda b, pt: (b, 0)),
                      pl.BlockSpec(memory_space=pl.ANY)],
            out_specs=pl.BlockSpec((1, Q, N), lambda b, pt: (b, 0, 0)),
            scratch_shapes=[pltpu.VMEM((2, IDX_DIM, past), F32),
                            pltpu.SemaphoreType.DMA((2, 1))]),
        out_shape=jax.ShapeDtypeStruct((nseq, Q, N), F32),
        compiler_params=_cparams(("arbitrary",)),
        name="dsa_sample_scores",
    )(page_table, ia, iwc, ik_new, ci_t)

    gb = 8 if nseq % 8 == 0 else 1
    bias = pl.pallas_call(
        functools.partial(_dsa_sample_select_kernel, topk=topk),
        grid=(nseq // gb,),
        in_specs=[pl.BlockSpec((gb, Q, N), lambda g: (g, 0, 0))],
        out_specs=pl.BlockSpec((gb, Q, N), lambda g: (g, 0, 0)),
        out_shape=jax.ShapeDtypeStruct((nseq, Q, N), F32),
        compiler_params=_cparams(("parallel",)),
        name="dsa_sample_select",
    )(keys)

    return pl.pallas_call(
        functools.partial(_dsa_sample_attend_kernel, layer=layer, n_pages=n_pages, Q=Q),
        grid_spec=pltpu.PrefetchScalarGridSpec(
            num_scalar_prefetch=1,
            grid=(nseq,),
            in_specs=[pl.BlockSpec((1, Q, N), lambda b, pt: (b, 0, 0)),
                      pl.BlockSpec((1, R, KD), lambda b, pt: (b, 0, 0)),
                      pl.BlockSpec((Q, KD), lambda b, pt: (b, 0)),
                      pl.BlockSpec((Q, KD), lambda b, pt: (b, 0)),
                      pl.BlockSpec(memory_space=pl.ANY),
                      pl.BlockSpec(memory_space=pl.ANY)],
            out_specs=pl.BlockSpec((1, R, HEAD_DIM), lambda b, pt: (b, 0, 0)),
            scratch_shapes=[pltpu.VMEM((2, KD, past), F32),
                            pltpu.VMEM((2, KD, past), F32),
                            pltpu.SemaphoreType.DMA((2, 2))]),
        out_shape=jax.ShapeDtypeStruct((nseq, R, HEAD_DIM), F32),
        compiler_params=_cparams(("arbitrary",)),
        name="dsa_sample_attend",
    )(page_table, bias, qblk, k_new, v_new, ck_t, cv_t)


def _even_weight(w):
    offs = np.cumsum((0,) + EVEN_COLS)
    seg = lambda j: w[:, offs[j]:offs[j + 1]]
    gq, gk, gv, gr, gg, dq, dk, dv, iq, ik, iw = [seg(j) for j in range(len(EVEN_COLS))]
    pad = jnp.zeros((w.shape[0], SMALL_COLS - IDX_DIM - GLA_GATE_RANK - IDX_HEADS), w.dtype)
    return jnp.concatenate([gq, gk, gv, gr, dq, iq, dk, dv, ik, gg, iw, pad], axis=1).astype(BF16)


def _rope_tables(pos, reps):
    half = HEAD_DIM // 2
    inv = ROPE_THETA ** (-jnp.arange(half, dtype=F32) / half)
    ang = pos.astype(F32)[:, None] * inv[None, :]
    cos = jnp.cos(ang)
    sin = jnp.sin(ang)
    cos = jnp.tile(jnp.concatenate([cos, cos], axis=1), (reps, LANES // HEAD_DIM))
    sin = jnp.tile(jnp.concatenate([-sin, sin], axis=1), (reps, LANES // HEAD_DIM))
    return cos, sin


def _block_diag_queries(q, nseq, Q, n_kv, group):
    q5 = q.reshape(nseq, Q, n_kv, group, HEAD_DIM).transpose(0, 2, 3, 1, 4)
    eye = jnp.eye(n_kv, dtype=q.dtype)
    blk = q5[:, :, :, :, None, :] * eye[None, :, None, None, :, None]
    return blk.reshape(nseq, n_kv * group * Q, n_kv * HEAD_DIM)


def _rows_to_tokens(o, nseq, Q, heads):
    return o.reshape(nseq, heads, Q, HEAD_DIM).transpose(0, 2, 1, 3).reshape(nseq * Q, heads * HEAD_DIM)


def kernel(x_prompt, x_sample, cache_k, cache_v, cache_idx, state_gla, state_swa_k, state_swa_v, page_table,
           w_in_even, gla_gate_w2, gla_gate_b, gla_norm_g, w_out_even, w_in_odd, swa_sinks, w_out_odd,
           ffn_w_gu, ffn_w_down, ln_g, ln_b):
    B, L, D = x_prompt.shape
    Bd, Q, _ = x_sample.shape
    depth = ffn_w_gu.shape[0]
    d_ff = ffn_w_down.shape[1]
    n_even, n_pool = cache_k.shape[0], cache_k.shape[1]
    alpha = (2.0 * depth) ** 0.25
    tm = 512 if (B * L) % 512 == 0 and (Bd * Q) % 512 == 0 else 128
    ff_chunk = 256
    seq_per_step = 4 if Bd % 4 == 0 else 1

    w_even = [_even_weight(w_in_even[i]) for i in range(n_even)]
    w_odd = w_in_odd.astype(BF16)
    wo_even = w_out_even.astype(BF16)
    wo_odd = w_out_odd.astype(BF16)
    w_gu = ffn_w_gu.astype(BF16)
    w_dn = ffn_w_down.astype(BF16)
    w2 = gla_gate_w2.astype(BF16)
    s0_prompt = jnp.zeros((1, B, GLA_HEADS, GLA_DK, GLA_DV), F32)

    ci_t = jnp.swapaxes(cache_idx, 2, 3)
    kd = DSA_KV_HEADS * HEAD_DIM
    ck_t = cache_k.transpose(0, 1, 3, 4, 2).reshape(n_even, n_pool, kd, PAGE_SIZE)
    cv_t = cache_v.transpose(0, 1, 3, 4, 2).reshape(n_even, n_pool, kd, PAGE_SIZE)
    skd = SWA_KV_HEADS * HEAD_DIM
    n_odd = state_swa_k.shape[0]
    swk_t = state_swa_k.transpose(0, 1, 3, 4, 2).reshape(n_odd, Bd, skd, WINDOW)
    swv_t = state_swa_v.transpose(0, 1, 3, 4, 2).reshape(n_odd, Bd, skd, WINDOW)

    cos_p, sin_p = _rope_tables(jnp.arange(L), 1)
    past_len = page_table.shape[1] * PAGE_SIZE
    cos_s, sin_s = _rope_tables(past_len + jnp.arange(Q), Bd)

    def finish_layer(l, x, lhs, w_out):
        ln = jnp.stack([ln_g[l, 0], ln_b[l, 0], ln_g[l, 1], ln_b[l, 1]])
        return _layer_tail(x, lhs, w_out, l // 2, ln, w_gu, w_dn, l, alpha, tm, ff_chunk)

    def even_common(i, x):
        return _project(x, w_even[i], EVEN_SPLITS, tm)

    x = x_prompt.reshape(B * L, D)
    ev_p, od_p = [], []
    for l in range(depth):
        i = l // 2
        if l % 2 == 0:
            gla, dq, iq, dk, dv, small = even_common(i, x)
            y_gla, s_fin = _gla(gla, small, s0_prompt, 0, w2, i, gla_gate_b[i][None], gla_norm_g[i][None],
                                nseq=B, seqlen=L, C=64, TB=256)
            iw_t = small[:, SM_IW:SM_IW + IDX_HEADS].T
            y_dsa = _dsa_prompt(dq, iq, iw_t, small, dk, dv, nseq=B, seqlen=L)
            ev_p.append((dk.reshape(B, L, DSA_KV_HEADS, HEAD_DIM), dv.reshape(B, L, DSA_KV_HEADS, HEAD_DIM),
                         small[:, SM_IK:SM_IK + IDX_DIM].reshape(B, L, IDX_DIM), s_fin))
            x = finish_layer(l, x, [y_gla, y_dsa], wo_even)
        else:
            q, k, v = _project_rope(x, w_odd, i, cos_p, sin_p, tm)
            sink_rows = jnp.repeat(swa_sinks[i], WINDOW).reshape(SWA_KV_HEADS, 1, SWA_GROUP * WINDOW)
            y = _swa_prompt(q, k, v, sink_rows, nseq=B, seqlen=L)
            k4 = k.reshape(B, L, SWA_KV_HEADS, HEAD_DIM)
            v4 = v.reshape(B, L, SWA_KV_HEADS, HEAD_DIM)
            od_p.append((k4[:, L - WINDOW:], v4[:, L - WINDOW:]))
            x = finish_layer(l, x, [y], wo_odd)
    y_prompt = x.reshape(B, L, D)

    x = x_sample.reshape(Bd * Q, D)
    ev_s, od_s = [], []
    for l in range(depth):
        i = l // 2
        if l % 2 == 0:
            gla, dq, iq, dk, dv, small = even_common(i, x)
            y_gla, s_fin = _gla(gla, small, state_gla, i, w2, i, gla_gate_b[i][None], gla_norm_g[i][None],
                                nseq=Bd, seqlen=Q, C=Q, TB=Q, NS=seq_per_step)
            ik_new = small[:, SM_IK:SM_IK + IDX_DIM]
            iw = small[:, SM_IW:SM_IW + IDX_HEADS]
            ia = iq.reshape(Bd, Q, IDX_HEADS, IDX_DIM).transpose(0, 2, 1, 3).reshape(Bd, IDX_HEADS * Q, IDX_DIM)
            iwc = iw.reshape(Bd, Q, IDX_HEADS).transpose(0, 2, 1).reshape(Bd, IDX_HEADS * Q, 1)
            qblk = _block_diag_queries(dq, Bd, Q, DSA_KV_HEADS, DSA_GROUP)
            o = _dsa_sample(page_table, ia, iwc, ik_new, qblk, dk, dv, ci_t, ck_t, cv_t, layer=i, nseq=Bd, Q=Q)
            y_dsa = _rows_to_tokens(o, Bd, Q, DSA_HEADS)
            ev_s.append((dk.reshape(Bd, Q, DSA_KV_HEADS, HEAD_DIM), dv.reshape(Bd, Q, DSA_KV_HEADS, HEAD_DIM),
                         ik_new.reshape(Bd, Q, IDX_DIM), s_fin))
            x = finish_layer(l, x, [y_gla, y_dsa], wo_even)
        else:
            q, k, v = _project_rope(x, w_odd, i, cos_s, sin_s, tm)
            qblk = _block_diag_queries(q, Bd, Q, SWA_KV_HEADS, SWA_GROUP)
            sink_col = jnp.repeat(swa_sinks[i], Q)[:, None]
            o, kt_new, vt_new = _swa_sample(qblk, k, v, swk_t, swv_t, i, sink_col, nseq=Bd, Q=Q, NS=seq_per_step)
            y = _rows_to_tokens(o, Bd, Q, SWA_HEADS)
            back = lambda t: t.reshape(Bd, SWA_KV_HEADS, HEAD_DIM, WINDOW).transpose(0, 3, 1, 2)
            od_s.append((back(kt_new), back(vt_new)))
            x = finish_layer(l, x, [y], wo_odd)
    y_sample = x.reshape(Bd, Q, D)

    stack = lambda states: [jnp.stack(z) for z in zip(*states)]
    k_p, v_p, idx_p, gla_p = stack(ev_p)
    swk_p, swv_p = stack(od_p)
    k_s, v_s, idx_s, gla_s = stack(ev_s)
    swk_s, swv_s = stack(od_s)
    return (y_prompt, y_sample, k_p, v_p, idx_p, gla_p, swk_p, swv_p, k_s, v_s, idx_s, gla_s, swk_s, swv_s)
```

```python
import functools

import jax
import jax.numpy as jnp
import numpy as np
from jax import lax
from jax.experimental import pallas as pl
from jax.experimental.pallas import tpu as pltpu

F32 = jnp.float32
BF16 = jnp.bfloat16
I32 = jnp.int32

GLA_HEADS = 4
GLA_DK = 64
GLA_DV = 128
GLA_GATE_RANK = 16
GLA_TAU = 16.0
DSA_HEADS = 8
DSA_KV_HEADS = 2
DSA_GROUP = DSA_HEADS // DSA_KV_HEADS
HEAD_DIM = 64
IDX_HEADS = 8
IDX_DIM = 64
DSA_TOPK = 256
DSA_QBLOCK = 128
SWA_HEADS = 16
SWA_KV_HEADS = 4
SWA_GROUP = SWA_HEADS // SWA_KV_HEADS
WINDOW = 128
ROPE_THETA = 150000.0
PAGE_SIZE = 128
LN_EPS = 1e-5
RMS_EPS = 1e-6

EVEN_COLS = (GLA_HEADS * GLA_DK, GLA_HEADS * GLA_DK, GLA_HEADS * GLA_DV, GLA_HEADS * GLA_DV, GLA_GATE_RANK,
             DSA_HEADS * HEAD_DIM, DSA_KV_HEADS * HEAD_DIM, DSA_KV_HEADS * HEAD_DIM,
             IDX_HEADS * IDX_DIM, IDX_DIM, IDX_HEADS)
GLA_COLS = 2 * GLA_HEADS * GLA_DK + 2 * GLA_HEADS * GLA_DV
DQ_COLS = DSA_HEADS * HEAD_DIM
IQ_COLS = IDX_HEADS * IDX_DIM
DKV_COLS = DSA_KV_HEADS * HEAD_DIM
SMALL_COLS = 128
SM_IK = 0
SM_GG = IDX_DIM
SM_IW = IDX_DIM + GLA_GATE_RANK
EVEN_SPLITS = (GLA_COLS, DQ_COLS, IQ_COLS, DKV_COLS, DKV_COLS, SMALL_COLS)

LANES = 128
SUBLANES = 8
VMEM_LIMIT_BYTES = 56 * 1024 * 1024

ONES_ROWS = 2 * SUBLANES
LOG2E = 1.4426950408889634
INT_MIN = np.int32(-(2 ** 31))
KEY_NEG_INF = np.int32(np.array(-np.inf, np.float32).view(np.int32) ^ np.int32(0x7FFFFFFF))
GLA_EXP_CLAMP = 80.0


def _cparams(sem):
    return pltpu.CompilerParams(dimension_semantics=sem, vmem_limit_bytes=VMEM_LIMIT_BYTES)


def _dot(a, b):
    return jnp.dot(a, b, preferred_element_type=F32)


def _dot_nt(a, b):
    return lax.dot_general(a, b, (((1,), (1,)), ((), ())), preferred_element_type=F32)


def _dot_tn(a, b):
    return lax.dot_general(a, b, (((0,), (0,)), ((), ())), preferred_element_type=F32)


def _layer_norm(v, g, b):
    mu = jnp.mean(v, axis=-1, keepdims=True)
    c = v - mu
    var = jnp.mean(c * c, axis=-1, keepdims=True)
    return c * lax.rsqrt(var + LN_EPS) * g + b


def _silu(v):
    return v * jax.nn.sigmoid(v)


def _key_float(key):
    bits = jnp.where(key < 0, key ^ np.int32(0x7FFFFFFF), key)
    return jnp.where(key < KEY_NEG_INF, F32(-jnp.inf), lax.bitcast_convert_type(bits, F32))


def _proj_kernel(x_ref, w_ref, *o_refs, splits):
    x = x_ref[...].astype(BF16)
    off = 0
    for o_ref, n in zip(o_refs, splits):
        o_ref[...] = _dot(x, w_ref[:, off:off + n])
        off += n


def _project(x, w, splits, tm):
    T, D = x.shape
    N = w.shape[1]
    return pl.pallas_call(
        functools.partial(_proj_kernel, splits=splits),
        grid=(T // tm,),
        in_specs=[pl.BlockSpec((tm, D), lambda i: (i, 0)),
                  pl.BlockSpec((D, N), lambda i: (0, 0))],
        out_specs=[pl.BlockSpec((tm, n), lambda i: (i, 0)) for n in splits],
        out_shape=[jax.ShapeDtypeStruct((T, n), F32) for n in splits],
        compiler_params=_cparams(("parallel",)),
        name="proj_even",
    )(x, w)


def _proj_rope_kernel(x_ref, w_ref, cos_ref, sin_ref, q_ref, k_ref, v_ref, *, nq, nk):
    x = x_ref[...].astype(BF16)
    cos = cos_ref[...]
    sin = sin_ref[...]
    half = HEAD_DIM // 2
    first = lax.broadcasted_iota(I32, cos.shape, 1) % HEAD_DIM < half

    def rope(u):
        swapped = jnp.where(first, pltpu.roll(u, LANES - half, axis=1), pltpu.roll(u, half, axis=1))
        return u * cos + swapped * sin

    for j in range(nq // LANES):
        q_ref[:, j * LANES:(j + 1) * LANES] = rope(_dot(x, w_ref[:, j * LANES:(j + 1) * LANES]))
    for j in range(nk // LANES):
        k_ref[:, j * LANES:(j + 1) * LANES] = rope(_dot(x, w_ref[:, nq + j * LANES:nq + (j + 1) * LANES]))
    v_ref[...] = _dot(x, w_ref[:, nq + nk:nq + 2 * nk])


def _project_rope(x, w, layer, cos, sin, tm):
    T, D = x.shape
    N = w.shape[2]
    nq, nk = SWA_HEADS * HEAD_DIM, SWA_KV_HEADS * HEAD_DIM
    ntab = cos.shape[0] // tm
    return pl.pallas_call(
        functools.partial(_proj_rope_kernel, nq=nq, nk=nk),
        grid=(T // tm,),
        in_specs=[pl.BlockSpec((tm, D), lambda i: (i, 0)),
                  pl.BlockSpec((None, D, N), lambda i: (layer, 0, 0)),
                  pl.BlockSpec((tm, LANES), lambda i: (i % ntab, 0)),
                  pl.BlockSpec((tm, LANES), lambda i: (i % ntab, 0))],
        out_specs=[pl.BlockSpec((tm, nq), lambda i: (i, 0)),
                   pl.BlockSpec((tm, nk), lambda i: (i, 0)),
                   pl.BlockSpec((tm, nk), lambda i: (i, 0))],
        out_shape=[jax.ShapeDtypeStruct((T, nq), F32),
                   jax.ShapeDtypeStruct((T, nk), F32),
                   jax.ShapeDtypeStruct((T, nk), F32)],
        compiler_params=_cparams(("parallel",)),
        name="proj_odd_rope",
    )(x, w, cos, sin)


def _layer_tail_kernel(*refs, n_lhs, alpha, dff, chunk):
    x_ref = refs[0]
    lhs = refs[1:1 + n_lhs]
    ws = refs[1 + n_lhs:1 + 2 * n_lhs]
    ln_ref, wgu_ref, wd_ref, o_ref = refs[1 + 2 * n_lhs:]
    acc = alpha * x_ref[...]
    for a_ref, w_ref in zip(lhs, ws):
        acc = acc + _dot(a_ref[...].astype(BF16), w_ref[...])
    h = _layer_norm(acc, ln_ref[0:1, :], ln_ref[1:2, :])
    hb = h.astype(BF16)
    acc = alpha * h
    for c in range(dff // chunk):
        gate = _dot(hb, wgu_ref[:, c * chunk:(c + 1) * chunk])
        up = _dot(hb, wgu_ref[:, dff + c * chunk:dff + (c + 1) * chunk])
        act = (_silu(gate) * up).astype(BF16)
        acc = acc + _dot(act, wd_ref[c * chunk:(c + 1) * chunk, :])
    o_ref[...] = _layer_norm(acc, ln_ref[2:3, :], ln_ref[3:4, :])


def _layer_tail(x, lhs, w_out, out_layer, ln, wgu, wd, layer, alpha, tm, chunk):
    T, D = x.shape
    n = len(lhs)
    dff = wd.shape[1]
    once = pl.Buffered(1)
    in_specs = [pl.BlockSpec((tm, D), lambda i: (i, 0))]
    in_specs += [pl.BlockSpec((tm, a.shape[1]), lambda i: (i, 0)) for a in lhs]
    in_specs += [pl.BlockSpec((None, a.shape[1], D), lambda i, r=r: (out_layer, r, 0), pipeline_mode=once)
                 for r, a in enumerate(lhs)]
    in_specs += [pl.BlockSpec(ln.shape, lambda i: (0, 0)),
                 pl.BlockSpec((None,) + wgu.shape[1:], lambda i: (layer, 0, 0), pipeline_mode=once),
                 pl.BlockSpec((None,) + wd.shape[1:], lambda i: (layer, 0, 0), pipeline_mode=once)]
    return pl.pallas_call(
        functools.partial(_layer_tail_kernel, n_lhs=n, alpha=alpha, dff=dff, chunk=chunk),
        grid=(T // tm,),
        in_specs=in_specs,
        out_specs=pl.BlockSpec((tm, D), lambda i: (i, 0)),
        out_shape=jax.ShapeDtypeStruct((T, D), F32),
        compiler_params=_cparams(("parallel",)),
        name="layer_tail",
    )(x, *lhs, *([w_out] * n), ln, wgu, wd)


def _gla_kernel(gla_ref, small_ref, s0_ref, w2_ref, gb_ref, ng_ref, y_ref, sout_ref, s_scr, *, C, TB, NS):
    H, DK, DV = GLA_HEADS, GLA_DK, GLA_DV
    j = pl.program_id(1)

    @pl.when(j == 0)
    def _():
        s_scr[...] = s0_ref[...]

    rowc = lax.broadcasted_iota(I32, (C, C), 0)
    colc = lax.broadcasted_iota(I32, (C, C), 1)
    tril = rowc >= colc
    tril_b = tril.astype(BF16)
    eye_dk = lax.broadcasted_iota(I32, (DK, DK), 0) == lax.broadcasted_iota(I32, (DK, DK), 1)
    w2 = w2_ref[...]
    gb = gb_ref[...]
    ng = ng_ref[...]
    mid = C // 2 - 1

    for sq, c in [(sq, c) for sq in range(NS) for c in range(TB // C)]:
        rows = slice(sq * TB + c * C, sq * TB + (c + 1) * C)
        gg = small_ref[rows, SM_GG:SM_GG + GLA_GATE_RANK].astype(BF16)
        z = _dot(gg, w2) + gb
        glog = -(jnp.maximum(-z, 0.0) + jnp.log1p(jnp.exp(-jnp.abs(z)))) * (1.0 / GLA_TAU)
        g1 = glog.astype(BF16)
        r1 = glog - g1.astype(F32)
        g2 = r1.astype(BF16)
        g3 = (r1 - g2.astype(F32)).astype(BF16)
        b = _dot(tril_b, g1) + _dot(tril_b, g2) + _dot(tril_b, g3)
        for h in range(H):
            qh = gla_ref[rows, h * DK:(h + 1) * DK] * (DK ** -0.5)
            kh = gla_ref[rows, H * DK + h * DK:H * DK + (h + 1) * DK]
            vh = gla_ref[rows, 2 * H * DK + h * DV:2 * H * DK + (h + 1) * DV].astype(BF16)
            rh = gla_ref[rows, 2 * H * DK + H * DV + h * DV:2 * H * DK + H * DV + (h + 1) * DV]
            bh = b[:, h * DK:(h + 1) * DK]
            bl = bh[C - 1:C, :]
            rr = bh[mid:mid + 1, :]
            qt = (qh * jnp.exp(jnp.clip(bh - rr, -GLA_EXP_CLAMP, GLA_EXP_CLAMP))).astype(BF16)
            kt = (kh * jnp.exp(jnp.clip(rr - bh, -GLA_EXP_CLAMP, GLA_EXP_CLAMP))).astype(BF16)
            a = jnp.where(tril, _dot_nt(qt, kt), 0.0)
            s_prev = s_scr[sq, h]
            o = _dot(a.astype(BF16), vh) + _dot((qh * jnp.exp(bh)).astype(BF16), s_prev.astype(BF16))
            kin = (kh * jnp.exp(bl - bh)).astype(BF16)
            kv = _dot_tn(kin, vh)
            dcol = jnp.sum(jnp.where(eye_dk, jnp.broadcast_to(bl, (DK, DK)), 0.0), axis=1, keepdims=True)
            s_scr[sq, h] = jnp.exp(dcol) * s_prev + kv
            ms = jnp.mean(o * o, axis=-1, keepdims=True)
            y_ref[rows, h * DV:(h + 1) * DV] = o * lax.rsqrt(ms + RMS_EPS) * ng * _silu(rh)

    @pl.when(j == pl.num_programs(1) - 1)
    def _():
        sout_ref[...] = s_scr[...]


def _gla(gla, small, s0, s0_layer, w2, layer, gb, ng, *, nseq, seqlen, C, TB, NS=1):
    H, DK, DV = GLA_HEADS, GLA_DK, GLA_DV
    T = gla.shape[0]
    nj = seqlen // TB
    assert NS == 1 or nj == 1
    rows = NS * TB
    return pl.pallas_call(
        functools.partial(_gla_kernel, C=C, TB=TB, NS=NS),
        grid=(nseq // NS, nj),
        in_specs=[pl.BlockSpec((rows, GLA_COLS), lambda b, j: (b * nj + j, 0)),
                  pl.BlockSpec((rows, SMALL_COLS), lambda b, j: (b * nj + j, 0)),
                  pl.BlockSpec((None, NS, H, DK, DV), lambda b, j: (s0_layer, b, 0, 0, 0)),
                  pl.BlockSpec((None,) + w2.shape[1:], lambda b, j: (layer, 0, 0)),
                  pl.BlockSpec(gb.shape, lambda b, j: (0, 0)),
                  pl.BlockSpec(ng.shape, lambda b, j: (0, 0))],
        out_specs=[pl.BlockSpec((rows, H * DV), lambda b, j: (b * nj + j, 0)),
                   pl.BlockSpec((NS, H, DK, DV), lambda b, j: (b, 0, 0, 0))],
        out_shape=[jax.ShapeDtypeStruct((T, H * DV), F32),
                   jax.ShapeDtypeStruct((nseq, H, DK, DV), F32)],
        scratch_shapes=[pltpu.VMEM((NS, H, DK, DV), F32)],
        compiler_params=_cparams(("parallel", "arbitrary")),
        name="gla",
    )(gla, small, s0, w2, gb, ng)


def _dsa_prompt_kernel(dq_ref, iq_ref, iwt_ref, small_ref, dk_ref, dvt_ref, y_ref,
                       sc_scr, bias_scr, acc_scr, m_scr, j_scr, sa_scr, sb_scr, ma_scr, mb_scr, *, L, topk, KC):
    QB = DSA_QBLOCK
    TPC = KC // QB
    G = DSA_GROUP
    i = pl.program_id(1)
    nch = (i + TPC) // TPC
    neg_inf = F32(-jnp.inf)
    row = lax.broadcasted_iota(I32, (QB, QB), 0)
    rowc = lax.broadcasted_iota(I32, (KC, QB), 0)
    qidx = i * QB + lax.broadcasted_iota(I32, (KC, QB), 1)

    iw_t = iwt_ref[...] * (IDX_HEADS ** -0.5 * IDX_DIM ** -0.5)
    iq_all = jnp.concatenate([iq_ref[:, h * IDX_DIM:(h + 1) * IDX_DIM] for h in range(IDX_HEADS)],
                             axis=0).astype(BF16)

    def score_chunk(c, carry):
        ks = pl.multiple_of(c * KC, KC)
        ik_c = small_ref[pl.ds(ks, KC), SM_IK:SM_IK + IDX_DIM].astype(BF16)
        d = _dot_nt(ik_c, iq_all)
        sc = jnp.maximum(d[:, 0:QB], 0.0) * iw_t[0:1, :]
        for h in range(1, IDX_HEADS):
            sc = sc + jnp.maximum(d[:, h * QB:(h + 1) * QB], 0.0) * iw_t[h:h + 1, :]
        sc_scr[pl.ds(ks, KC), :] = jnp.where(ks + rowc <= qidx, sc, neg_inf)
        return carry

    lax.fori_loop(0, nch, score_chunk, 0)

    def count(pred):
        def body(c, part):
            for t in range(TPC):
                ks = pl.multiple_of(c * KC + t * QB, QB)
                m = pred(sc_scr[pl.ds(ks, QB), :], ks).astype(I32)
                part = part + m.reshape(QB // SUBLANES, SUBLANES, QB).sum(axis=0)
            return part
        part = lax.fori_loop(0, nch, body, jnp.zeros((SUBLANES, QB), I32))
        return part.sum(axis=0, keepdims=True)

    def search_bit(p, tkey):
        cand_key = tkey + jnp.left_shift(I32(1), 31 - p)
        cand = _key_float(cand_key)
        cnt = count(lambda s, ks: s >= cand)
        return jnp.where(cnt >= topk, cand_key, tkey)

    thr_key = lax.fori_loop(0, 32, search_bit, jnp.full((1, QB), INT_MIN, I32))
    thr = _key_float(thr_key)

    need = topk - count(lambda s, ks: s > thr)
    n_eq = count(lambda s, ks: s == thr)
    excess = (n_eq > need) & (thr > neg_inf)
    j_scr[...] = jnp.full((1, QB), L, I32)
    nbits = int(L).bit_length()

    @pl.when(jnp.max(excess.astype(I32)) > 0)
    def _():
        def search_idx(p, lim):
            cand = lim + jnp.left_shift(I32(1), nbits - 1 - p)
            cnt = count(lambda t, ks: (t == thr) & (ks + row < cand))
            return jnp.where(cnt < need, cand, lim)
        lim = lax.fori_loop(0, nbits, search_idx, jnp.zeros((1, QB), I32))
        j_scr[...] = jnp.where(excess, lim, L)

    last_tie = j_scr[...]

    def bias_chunk(c, carry):
        for t in range(TPC):
            ks = pl.multiple_of(c * KC + t * QB, QB)
            s = sc_scr[pl.ds(ks, QB), :]
            sel = (s > thr) | ((s == thr) & (ks + row <= last_tie))
            bias_scr[pl.ds(ks, QB), :] = jnp.where(sel & (s > neg_inf), 0.0, neg_inf)
        return carry

    lax.fori_loop(0, nch, bias_chunk, 0)

    m_scr[...] = jnp.full(m_scr.shape, -1e30, F32)
    acc_scr[...] = jnp.zeros(acc_scr.shape, F32)
    zq = jnp.zeros((QB, HEAD_DIM), F32)
    q_rows = []
    for h in range(DSA_HEADS):
        qh = dq_ref[:, h * HEAD_DIM:(h + 1) * HEAD_DIM] * (HEAD_DIM ** -0.5 * LOG2E)
        parts = [qh if n == h // G else zq for n in range(DSA_KV_HEADS)]
        q_rows.append(jnp.concatenate(parts, axis=1))
    q_blk = jnp.concatenate(q_rows, axis=0).astype(BF16)
    ones_rows = jnp.ones((ONES_ROWS, KC), BF16)

    def scores(c, s_ref, cmax_ref):
        ks = pl.multiple_of(c * KC, KC)
        bias = bias_scr[pl.ds(ks, KC), :]
        k_c = dk_ref[pl.ds(ks, KC), :].astype(BF16)
        s = _dot_nt(k_c, q_blk) + jnp.concatenate([bias] * DSA_HEADS, axis=1)
        s_ref[...] = s
        cmax_ref[...] = jnp.max(s, axis=0, keepdims=True)

    def accumulate(c, s_ref, cmax_ref):
        vt_c = dvt_ref[c].astype(BF16)
        m_old = m_scr[...]
        m_new = jnp.maximum(m_old, cmax_ref[...])
        alpha = jnp.exp2(m_old - m_new)
        pb = jnp.exp2(s_ref[...] - m_new).astype(BF16)
        gw = G * QB
        for n in range(DSA_KV_HEADS):
            cols = slice(n * gw, (n + 1) * gw)
            vt1 = jnp.concatenate([vt_c[n * HEAD_DIM:(n + 1) * HEAD_DIM, :], ones_rows], axis=0)
            acc_scr[:, cols] = alpha[:, cols] * acc_scr[:, cols] + _dot(vt1, pb[:, cols])
        m_scr[...] = m_new

    def attend_pair(j, carry):
        c0 = 2 * j
        scores(jnp.minimum(c0 + 1, nch - 1), sb_scr, mb_scr)
        accumulate(c0, sa_scr, ma_scr)

        @pl.when(c0 + 1 < nch)
        def _():
            scores(jnp.minimum(c0 + 2, nch - 1), sa_scr, ma_scr)
            accumulate(c0 + 1, sb_scr, mb_scr)
        return carry

    scores(0, sa_scr, ma_scr)
    lax.fori_loop(0, (nch + 1) // 2, attend_pair, 0)

    o_t = acc_scr[0:HEAD_DIM, :] / acc_scr[HEAD_DIM:HEAD_DIM + 1, :]
    y_ref[...] = jnp.concatenate([o_t[:, h * QB:(h + 1) * QB].T for h in range(DSA_HEADS)], axis=1)


def _dsa_prompt(dq, iq, iw_t, small, dk, dv, *, nseq, seqlen):
    QB = DSA_QBLOCK
    T = dq.shape[0]
    nb = seqlen // QB
    topk = min(DSA_TOPK, seqlen // 4)
    kc = 4 * QB if seqlen % (4 * QB) == 0 else QB
    ncs = seqlen // kc
    nh = DSA_HEADS * QB
    dvt = dv.reshape(nseq * ncs, kc, DKV_COLS).transpose(0, 2, 1)
    return pl.pallas_call(
        functools.partial(_dsa_prompt_kernel, L=seqlen, topk=topk, KC=kc),
        grid=(nseq, nb),
        in_specs=[pl.BlockSpec((QB, DQ_COLS), lambda b, i: (b * nb + i, 0)),
                  pl.BlockSpec((QB, IQ_COLS), lambda b, i: (b * nb + i, 0)),
                  pl.BlockSpec((IDX_HEADS, QB), lambda b, i: (0, b * nb + i)),
                  pl.BlockSpec((seqlen, SMALL_COLS), lambda b, i: (b, 0)),
                  pl.BlockSpec((seqlen, DKV_COLS), lambda b, i: (b, 0)),
                  pl.BlockSpec((ncs, DKV_COLS, kc), lambda b, i: (b, 0, 0))],
        out_specs=pl.BlockSpec((QB, DQ_COLS), lambda b, i: (b * nb + i, 0)),
        out_shape=jax.ShapeDtypeStruct((T, DQ_COLS), F32),
        scratch_shapes=[pltpu.VMEM((seqlen, QB), F32),
                        pltpu.VMEM((seqlen, QB), F32),
                        pltpu.VMEM((HEAD_DIM + ONES_ROWS, nh), F32),
                        pltpu.VMEM((1, nh), F32),
                        pltpu.VMEM((1, QB), I32),
                        pltpu.VMEM((kc, nh), F32),
                        pltpu.VMEM((kc, nh), F32),
                        pltpu.VMEM((1, nh), F32),
                        pltpu.VMEM((1, nh), F32)],
        compiler_params=_cparams(("parallel", "arbitrary")),
        name="dsa_prompt",
    )(dq, iq, iw_t, small, dk, dvt)


def _swa_prompt_kernel(q_ref, kp_ref, kc_ref, vtp_ref, vtc_ref, sink_ref, y_ref):
    W = WINDOW
    G = SWA_GROUP
    i = pl.program_id(1)
    c = lax.broadcasted_iota(I32, (2 * W, W), 0)
    a = lax.broadcasted_iota(I32, (2 * W, W), 1)
    mask = (c > a) & (c <= a + W) & ((i > 0) | (c >= W))
    bias = jnp.where(mask, 0.0, F32(-jnp.inf))
    bias = jnp.concatenate([bias] * G, axis=1)
    kk = jnp.concatenate([kp_ref[...], kc_ref[...]], axis=0).astype(BF16)
    vt = jnp.concatenate([vtp_ref[...], vtc_ref[...]], axis=1).astype(BF16)
    ones_rows = jnp.ones((ONES_ROWS, 2 * W), BF16)
    outs = []
    for n in range(SWA_KV_HEADS):
        ns = slice(n * HEAD_DIM, (n + 1) * HEAD_DIM)
        qn = jnp.concatenate([q_ref[:, (n * G + g) * HEAD_DIM:(n * G + g + 1) * HEAD_DIM] for g in range(G)], axis=0)
        qn = (qn * (HEAD_DIM ** -0.5 * LOG2E)).astype(BF16)
        s = _dot_nt(kk[:, ns], qn) + bias
        sink = sink_ref[n] * LOG2E
        m = jnp.maximum(jnp.max(s, axis=0, keepdims=True), sink)
        p = jnp.exp2(s - m).astype(BF16)
        vt1 = jnp.concatenate([vt[ns, :], ones_rows], axis=0)
        pv = _dot(vt1, p)
        o_t = pv[0:HEAD_DIM] / (pv[HEAD_DIM:HEAD_DIM + 1] + jnp.exp2(sink - m))
        outs += [o_t[:, g * W:(g + 1) * W].T for g in range(G)]
    y_ref[...] = jnp.concatenate(outs, axis=1)


def _swa_prompt(q, k, v, sinks, *, nseq, seqlen):
    W = WINDOW
    T = q.shape[0]
    nb = seqlen // W
    nq, nk = SWA_HEADS * HEAD_DIM, SWA_KV_HEADS * HEAD_DIM
    cur = lambda b, i: (b * nb + i, 0)
    prev = lambda b, i: (b * nb + jnp.maximum(i - 1, 0), 0)
    cur_t = lambda b, i: (0, b * nb + i)
    prev_t = lambda b, i: (0, b * nb + jnp.maximum(i - 1, 0))
    vt = v.T
    return pl.pallas_call(
        _swa_prompt_kernel,
        grid=(nseq, nb),
        in_specs=[pl.BlockSpec((W, nq), cur),
                  pl.BlockSpec((W, nk), prev), pl.BlockSpec((W, nk), cur),
                  pl.BlockSpec((nk, W), prev_t), pl.BlockSpec((nk, W), cur_t),
                  pl.BlockSpec(sinks.shape, lambda b, i: (0, 0, 0))],
        out_specs=pl.BlockSpec((W, nq), cur),
        out_shape=jax.ShapeDtypeStruct((T, nq), F32),
        compiler_params=_cparams(("parallel", "parallel")),
        name="swa_prompt",
    )(q, k, k, vt, vt, sinks)


def _swa_sample_kernel(qb_ref, kn_ref, vn_ref, kt_ref, vt_ref, sink_ref, o_ref, kto_ref, vto_ref, *, Q, NS):
    W = WINDOW
    R = SWA_HEADS * Q
    KD = SWA_KV_HEADS * HEAD_DIM
    neg_inf = F32(-jnp.inf)
    lane = lax.broadcasted_iota(I32, (KD, W), 1)
    qi = lax.broadcasted_iota(I32, (R, W), 0) % Q
    cc = lax.broadcasted_iota(I32, (R, W), 1)
    sink = sink_ref[...]
    rpk = SWA_GROUP * Q

    def new_cols_t(x):
        pad = jnp.concatenate([x, jnp.zeros((W - Q, KD), F32)], axis=0)
        return pltpu.roll(pad.T, W - Q, axis=1)

    for sq in range(NS):
        kt = kt_ref[sq]
        vt = vt_ref[sq]
        knt = new_cols_t(kn_ref[sq * Q:(sq + 1) * Q, :])
        vnt = new_cols_t(vn_ref[sq * Q:(sq + 1) * Q, :])
        kto_ref[sq] = jnp.where(lane >= W - Q, knt, pltpu.roll(kt, W - Q, axis=1))
        vto_ref[sq] = jnp.where(lane >= W - Q, vnt, pltpu.roll(vt, W - Q, axis=1))

        qb = (qb_ref[sq] * (HEAD_DIM ** -0.5)).astype(BF16)
        s_buf = jnp.where(cc > qi, _dot(qb, kt.astype(BF16)), neg_inf)
        s_new = jnp.where((cc >= W - Q) & (cc - (W - Q) <= qi), _dot(qb, knt.astype(BF16)), neg_inf)
        m = jnp.maximum(jnp.maximum(jnp.max(s_buf, axis=-1, keepdims=True),
                                    jnp.max(s_new, axis=-1, keepdims=True)), sink)
        p_buf = jnp.exp(s_buf - m)
        p_new = jnp.exp(s_new - m)
        den = jnp.sum(p_buf, axis=-1, keepdims=True) + jnp.sum(p_new, axis=-1, keepdims=True) + jnp.exp(sink - m)
        o = (_dot_nt(p_buf.astype(BF16), vt.astype(BF16)) + _dot_nt(p_new.astype(BF16), vnt.astype(BF16))) / den
        o_ref[sq] = jnp.concatenate(
            [o[n * rpk:(n + 1) * rpk, n * HEAD_DIM:(n + 1) * HEAD_DIM] for n in range(SWA_KV_HEADS)], axis=0)


def _swa_sample(qblk, k_new, v_new, kt, vt, layer, sink_col, *, nseq, Q, NS=1):
    W = WINDOW
    R = SWA_HEADS * Q
    KD = SWA_KV_HEADS * HEAD_DIM
    return pl.pallas_call(
        functools.partial(_swa_sample_kernel, Q=Q, NS=NS),
        grid=(nseq // NS,),
        in_specs=[pl.BlockSpec((NS, R, KD), lambda b: (b, 0, 0)),
                  pl.BlockSpec((NS * Q, KD), lambda b: (b, 0)),
                  pl.BlockSpec((NS * Q, KD), lambda b: (b, 0)),
                  pl.BlockSpec((None, NS, KD, W), lambda b: (layer, b, 0, 0)),
                  pl.BlockSpec((None, NS, KD, W), lambda b: (layer, b, 0, 0)),
                  pl.BlockSpec((R, 1), lambda b: (0, 0))],
        out_specs=[pl.BlockSpec((NS, R, HEAD_DIM), lambda b: (b, 0, 0)),
                   pl.BlockSpec((NS, KD, W), lambda b: (b, 0, 0)),
                   pl.BlockSpec((NS, KD, W), lambda b: (b, 0, 0))],
        out_shape=[jax.ShapeDtypeStruct((nseq, R, HEAD_DIM), F32),
                   jax.ShapeDtypeStruct((nseq, KD, W), F32),
                   jax.ShapeDtypeStruct((nseq, KD, W), F32)],
        compiler_params=_cparams(("parallel",)),
        name="swa_sample",
    )(qblk, k_new, v_new, kt, vt, sink_col)


def _fetch_pages(pt_ref, srcs, bufs, sem, *, layer, n_pages):
    P = PAGE_SIZE
    b = pl.program_id(0)
    nb = pl.num_programs(0)

    def copies(bb, slot):
        out = []
        for p in range(n_pages):
            page = pt_ref[bb, p]
            for j, (src, buf) in enumerate(zip(srcs, bufs)):
                out.append(pltpu.make_async_copy(src.at[layer, page], buf.at[slot, :, p * P:(p + 1) * P],
                                                 sem.at[slot, j]))
        return out

    slot = b % 2

    @pl.when(b == 0)
    def _():
        for cp in copies(0, 0):
            cp.start()

    @pl.when(b + 1 < nb)
    def _():
        for cp in copies(b + 1, 1 - slot):
            cp.start()

    for cp in copies(b, slot):
        cp.wait()
    return slot


def _dsa_sample_scores_kernel(pt_ref, ia_ref, iwc_ref, ikn_ref, ci_ref, key_ref, ibuf, sem, *, layer, n_pages, Q):
    P = PAGE_SIZE
    past = n_pages * P
    neg_inf = F32(-jnp.inf)
    slot = _fetch_pages(pt_ref, [ci_ref], [ibuf], sem, layer=layer, n_pages=n_pages)

    ia = ia_ref[0].astype(BF16)
    iwc = iwc_ref[0] * (IDX_HEADS ** -0.5 * IDX_DIM ** -0.5)

    def head_sum(d):
        d = jnp.maximum(d, 0.0) * iwc
        acc = d[0:Q]
        for h in range(1, IDX_HEADS):
            acc = acc + d[h * Q:(h + 1) * Q]
        return acc

    sc_past = head_sum(_dot(ia, ibuf[slot].astype(BF16)))
    ikn = jnp.concatenate([ikn_ref[...], jnp.zeros((P - Q, IDX_DIM), F32)], axis=0)
    sc_new = head_sum(_dot_nt(ia, ikn.astype(BF16)))
    qrow = lax.broadcasted_iota(I32, (Q, P), 0)
    lane = lax.broadcasted_iota(I32, (Q, P), 1)
    sc_new = jnp.where(lane <= qrow, sc_new, neg_inf)
    key_ref[0, :, 0:past] = sc_past
    key_ref[0, :, past:past + P] = sc_new


def _dsa_sample_select_kernel(key_ref, bias_ref, *, topk):
    GB, Q, N = key_ref.shape
    R = GB * Q
    keys = key_ref[...].reshape(R, N)
    idx = lax.broadcasted_iota(I32, (R, N), 1)

    def lane_count(mask):
        mi = mask.astype(I32)
        part = mi[:, 0:LANES]
        for t in range(1, N // LANES):
            part = part + mi[:, t * LANES:(t + 1) * LANES]
        return jnp.sum(part, axis=1, keepdims=True)

    def search_bit(p, tkey):
        cand_key = tkey + jnp.left_shift(I32(1), 31 - p)
        return jnp.where(lane_count(keys >= _key_float(cand_key)) >= topk, cand_key, tkey)

    thr = _key_float(lax.fori_loop(0, 32, search_bit, jnp.full((R, 1), INT_MIN, I32)))

    neg_inf = F32(-jnp.inf)
    need = topk - lane_count(keys > thr)
    n_eq = lane_count(keys == thr)
    excess = (n_eq > need) & (thr > neg_inf)
    nbits = int(N).bit_length()

    def search_idx(p, lim):
        cand = lim + jnp.left_shift(I32(1), nbits - 1 - p)
        return jnp.where(lane_count((keys == thr) & (idx < cand)) < need, cand, lim)

    lim = lax.cond(jnp.max(excess.astype(I32)) > 0,
                   lambda: lax.fori_loop(0, nbits, search_idx, jnp.zeros((R, 1), I32)),
                   lambda: jnp.zeros((R, 1), I32))
    last_tie = jnp.where(excess, lim, N)
    sel = ((keys > thr) | ((keys == thr) & (idx <= last_tie))) & (keys > neg_inf)
    bias_ref[...] = jnp.where(sel, 0.0, neg_inf).reshape(GB, Q, N)


def _dsa_sample_attend_kernel(pt_ref, bias_ref, qb_ref, kn_ref, vn_ref, ck_ref, cv_ref, o_ref,
                              kbuf, vbuf, sem, *, layer, n_pages, Q):
    P = PAGE_SIZE
    past = n_pages * P
    KD = DSA_KV_HEADS * HEAD_DIM
    slot = _fetch_pages(pt_ref, [ck_ref, cv_ref], [kbuf, vbuf], sem, layer=layer, n_pages=n_pages)
    bias_past = bias_ref[0, :, 0:past]
    bias_new = bias_ref[0, :, past:past + P]

    qb = (qb_ref[0] * (HEAD_DIM ** -0.5)).astype(BF16)
    kn = jnp.concatenate([kn_ref[...], jnp.zeros((P - Q, KD), F32)], axis=0).astype(BF16)
    vn = jnp.concatenate([vn_ref[...], jnp.zeros((P - Q, KD), F32)], axis=0).astype(BF16)
    s_past = _dot(qb, kbuf[slot].astype(BF16)) + jnp.concatenate([bias_past] * DSA_HEADS, axis=0)
    s_new = _dot_nt(qb, kn) + jnp.concatenate([bias_new] * DSA_HEADS, axis=0)
    m = jnp.maximum(jnp.max(s_past, axis=-1, keepdims=True), jnp.max(s_new, axis=-1, keepdims=True))
    p_past = jnp.exp(s_past - m)
    p_new = jnp.exp(s_new - m)
    den = jnp.sum(p_past, axis=-1, keepdims=True) + jnp.sum(p_new, axis=-1, keepdims=True)
    o = (_dot_nt(p_past.astype(BF16), vbuf[slot].astype(BF16)) + _dot(p_new.astype(BF16), vn)) / den
    rpk = DSA_GROUP * Q
    o_ref[0] = jnp.concatenate(
        [o[n * rpk:(n + 1) * rpk, n * HEAD_DIM:(n + 1) * HEAD_DIM] for n in range(DSA_KV_HEADS)], axis=0)


def _dsa_sample(page_table, ia, iwc, ik_new, qblk, k_new, v_new, ci_t, ck_t, cv_t, *, layer, nseq, Q):
    P = PAGE_SIZE
    n_pages = page_table.shape[1]
    past = n_pages * P
    KD = DSA_KV_HEADS * HEAD_DIM
    R = DSA_HEADS * Q
    RI = IDX_HEADS * Q
    topk = min(DSA_TOPK, (past + Q) // 4)
    N = past + P
    keys = pl.pallas_call(
        functools.partial(_dsa_sample_scores_kernel, layer=layer, n_pages=n_pages, Q=Q),
        grid_spec=pltpu.PrefetchScalarGridSpec(
            num_scalar_prefetch=1,
            grid=(nseq,),
            in_specs=[pl.BlockSpec((1, RI, IDX_DIM), lambda b, pt: (b, 0, 0)),
                      pl.BlockSpec((1, RI, 1), lambda b, pt: (b, 0, 0)),
                      pl.BlockSpec((Q, IDX_DIM), lambda b, pt: (b, 0)),
                      pl.BlockSpec(memory_space=pl.ANY)],
            out_specs=pl.BlockSpec((1, Q, N), lambda b, pt: (b, 0, 0)),
            scratch_shapes=[pltpu.VMEM((2, IDX_DIM, past), F32),
                            pltpu.SemaphoreType.DMA((2, 1))]),
        out_shape=jax.ShapeDtypeStruct((nseq, Q, N), F32),
        compiler_params=_cparams(("arbitrary",)),
        name="dsa_sample_scores",
    )(page_table, ia, iwc, ik_new, ci_t)

    gb = 8 if nseq % 8 == 0 else 1
    bias = pl.pallas_call(
        functools.partial(_dsa_sample_select_kernel, topk=topk),
        grid=(nseq // gb,),
        in_specs=[pl.BlockSpec((gb, Q, N), lambda g: (g, 0, 0))],
        out_specs=pl.BlockSpec((gb, Q, N), lambda g: (g, 0, 0)),
        out_shape=jax.ShapeDtypeStruct((nseq, Q, N), F32),
        compiler_params=_cparams(("parallel",)),
        name="dsa_sample_select",
    )(keys)

    return pl.pallas_call(
        functools.partial(_dsa_sample_attend_kernel, layer=layer, n_pages=n_pages, Q=Q),
        grid_spec=pltpu.PrefetchScalarGridSpec(
            num_scalar_prefetch=1,
            grid=(nseq,),
            in_specs=[pl.BlockSpec((1, Q, N), lambda b, pt: (b, 0, 0)),
                      pl.BlockSpec((1, R, KD), lambda b, pt: (b, 0, 0)),
                      pl.BlockSpec((Q, KD), lambda b, pt: (b, 0)),
                      pl.BlockSpec((Q, KD), lambda b, pt: (b, 0)),
                      pl.BlockSpec(memory_space=pl.ANY),
                      pl.BlockSpec(memory_space=pl.ANY)],
            out_specs=pl.BlockSpec((1, R, HEAD_DIM), lambda b, pt: (b, 0, 0)),
            scratch_shapes=[pltpu.VMEM((2, KD, past), F32),
                            pltpu.VMEM((2, KD, past), F32),
                            pltpu.SemaphoreType.DMA((2, 2))]),
        out_shape=jax.ShapeDtypeStruct((nseq, R, HEAD_DIM), F32),
        compiler_params=_cparams(("arbitrary",)),
        name="dsa_sample_attend",
    )(page_table, bias, qblk, k_new, v_new, ck_t, cv_t)


def _even_weight(w):
    offs = np.cumsum((0,) + EVEN_COLS)
    seg = lambda j: w[:, offs[j]:offs[j + 1]]
    gq, gk, gv, gr, gg, dq, dk, dv, iq, ik, iw = [seg(j) for j in range(len(EVEN_COLS))]
    pad = jnp.zeros((w.shape[0], SMALL_COLS - IDX_DIM - GLA_GATE_RANK - IDX_HEADS), w.dtype)
    return jnp.concatenate([gq, gk, gv, gr, dq, iq, dk, dv, ik, gg, iw, pad], axis=1).astype(BF16)


def _rope_tables(pos, reps):
    half = HEAD_DIM // 2
    inv = ROPE_THETA ** (-jnp.arange(half, dtype=F32) / half)
    ang = pos.astype(F32)[:, None] * inv[None, :]
    cos = jnp.cos(ang)
    sin = jnp.sin(ang)
    cos = jnp.tile(jnp.concatenate([cos, cos], axis=1), (reps, LANES // HEAD_DIM))
    sin = jnp.tile(jnp.concatenate([-sin, sin], axis=1), (reps, LANES // HEAD_DIM))
    return cos, sin


def _block_diag_queries(q, nseq, Q, n_kv, group):
    q5 = q.reshape(nseq, Q, n_kv, group, HEAD_DIM).transpose(0, 2, 3, 1, 4)
    eye = jnp.eye(n_kv, dtype=q.dtype)
    blk = q5[:, :, :, :, None, :] * eye[None, :, None, None, :, None]
    return blk.reshape(nseq, n_kv * group * Q, n_kv * HEAD_DIM)


def _rows_to_tokens(o, nseq, Q, heads):
    return o.reshape(nseq, heads, Q, HEAD_DIM).transpose(0, 2, 1, 3).reshape(nseq * Q, heads * HEAD_DIM)


def kernel(x_prompt, x_sample, cache_k, cache_v, cache_idx, state_gla, state_swa_k, state_swa_v, page_table,
           w_in_even, gla_gate_w2, gla_gate_b, gla_norm_g, w_out_even, w_in_odd, swa_sinks, w_out_odd,
           ffn_w_gu, ffn_w_down, ln_g, ln_b):
    B, L, D = x_prompt.shape
    Bd, Q, _ = x_sample.shape
    depth = ffn_w_gu.shape[0]
    d_ff = ffn_w_down.shape[1]
    n_even, n_pool = cache_k.shape[0], cache_k.shape[1]
    alpha = (2.0 * depth) ** 0.25
    tm = 512 if (B * L) % 512 == 0 and (Bd * Q) % 512 == 0 else 128
    ff_chunk = 256
    seq_per_step = 4 if Bd % 4 == 0 else 1

    w_even = [_even_weight(w_in_even[i]) for i in range(n_even)]
    w_odd = w_in_odd.astype(BF16)
    wo_even = w_out_even.astype(BF16)
    wo_odd = w_out_odd.astype(BF16)
    w_gu = ffn_w_gu.astype(BF16)
    w_dn = ffn_w_down.astype(BF16)
    w2 = gla_gate_w2.astype(BF16)
    s0_prompt = jnp.zeros((1, B, GLA_HEADS, GLA_DK, GLA_DV), F32)

    ci_t = jnp.swapaxes(cache_idx, 2, 3)
    kd = DSA_KV_HEADS * HEAD_DIM
    ck_t = cache_k.transpose(0, 1, 3, 4, 2).reshape(n_even, n_pool, kd, PAGE_SIZE)
    cv_t = cache_v.transpose(0, 1, 3, 4, 2).reshape(n_even, n_pool, kd, PAGE_SIZE)
    skd = SWA_KV_HEADS * HEAD_DIM
    n_odd = state_swa_k.shape[0]
    swk_t = state_swa_k.transpose(0, 1, 3, 4, 2).reshape(n_odd, Bd, skd, WINDOW)
    swv_t = state_swa_v.transpose(0, 1, 3, 4, 2).reshape(n_odd, Bd, skd, WINDOW)

    cos_p, sin_p = _rope_tables(jnp.arange(L), 1)
    past_len = page_table.shape[1] * PAGE_SIZE
    cos_s, sin_s = _rope_tables(past_len + jnp.arange(Q), Bd)

    def finish_layer(l, x, lhs, w_out):
        ln = jnp.stack([ln_g[l, 0], ln_b[l, 0], ln_g[l, 1], ln_b[l, 1]])
        return _layer_tail(x, lhs, w_out, l // 2, ln, w_gu, w_dn, l, alpha, tm, ff_chunk)

    def even_common(i, x):
        return _project(x, w_even[i], EVEN_SPLITS, tm)

    x = x_prompt.reshape(B * L, D)
    ev_p, od_p = [], []
    for l in range(depth):
        i = l // 2
        if l % 2 == 0:
            gla, dq, iq, dk, dv, small = even_common(i, x)
            y_gla, s_fin = _gla(gla, small, s0_prompt, 0, w2, i, gla_gate_b[i][None], gla_norm_g[i][None],
                                nseq=B, seqlen=L, C=64, TB=256)
            iw_t = small[:, SM_IW:SM_IW + IDX_HEADS].T
            y_dsa = _dsa_prompt(dq, iq, iw_t, small, dk, dv, nseq=B, seqlen=L)
            ev_p.append((dk.reshape(B, L, DSA_KV_HEADS, HEAD_DIM), dv.reshape(B, L, DSA_KV_HEADS, HEAD_DIM),
                         small[:, SM_IK:SM_IK + IDX_DIM].reshape(B, L, IDX_DIM), s_fin))
            x = finish_layer(l, x, [y_gla, y_dsa], wo_even)
        else:
            q, k, v = _project_rope(x, w_odd, i, cos_p, sin_p, tm)
            sink_rows = jnp.repeat(swa_sinks[i], WINDOW).reshape(SWA_KV_HEADS, 1, SWA_GROUP * WINDOW)
            y = _swa_prompt(q, k, v, sink_rows, nseq=B, seqlen=L)
            k4 = k.reshape(B, L, SWA_KV_HEADS, HEAD_DIM)
            v4 = v.reshape(B, L, SWA_KV_HEADS, HEAD_DIM)
            od_p.append((k4[:, L - WINDOW:], v4[:, L - WINDOW:]))
            x = finish_layer(l, x, [y], wo_odd)
    y_prompt = x.reshape(B, L, D)

    x = x_sample.reshape(Bd * Q, D)
    ev_s, od_s = [], []
    for l in range(depth):
        i = l // 2
        if l % 2 == 0:
            gla, dq, iq, dk, dv, small = even_common(i, x)
            y_gla, s_fin = _gla(gla, small, state_gla, i, w2, i, gla_gate_b[i][None], gla_norm_g[i][None],
                                nseq=Bd, seqlen=Q, C=Q, TB=Q, NS=seq_per_step)
            ik_new = small[:, SM_IK:SM_IK + IDX_DIM]
            iw = small[:, SM_IW:SM_IW + IDX_HEADS]
            ia = iq.reshape(Bd, Q, IDX_HEADS, IDX_DIM).transpose(0, 2, 1, 3).reshape(Bd, IDX_HEADS * Q, IDX_DIM)
            iwc = iw.reshape(Bd, Q, IDX_HEADS).transpose(0, 2, 1).reshape(Bd, IDX_HEADS * Q, 1)
            qblk = _block_diag_queries(dq, Bd, Q, DSA_KV_HEADS, DSA_GROUP)
            o = _dsa_sample(page_table, ia, iwc, ik_new, qblk, dk, dv, ci_t, ck_t, cv_t, layer=i, nseq=Bd, Q=Q)
            y_dsa = _rows_to_tokens(o, Bd, Q, DSA_HEADS)
            ev_s.append((dk.reshape(Bd, Q, DSA_KV_HEADS, HEAD_DIM), dv.reshape(Bd, Q, DSA_KV_HEADS, HEAD_DIM),
                         ik_new.reshape(Bd, Q, IDX_DIM), s_fin))
            x = finish_layer(l, x, [y_gla, y_dsa], wo_even)
        else:
            q, k, v = _project_rope(x, w_odd, i, cos_s, sin_s, tm)
            qblk = _block_diag_queries(q, Bd, Q, SWA_KV_HEADS, SWA_GROUP)
            sink_col = jnp.repeat(swa_sinks[i], Q)[:, None]
            o, kt_new, vt_new = _swa_sample(qblk, k, v, swk_t, swv_t, i, sink_col, nseq=Bd, Q=Q, NS=seq_per_step)
            y = _rows_to_tokens(o, Bd, Q, SWA_HEADS)
            back = lambda t: t.reshape(Bd, SWA_KV_HEADS, HEAD_DIM, WINDOW).transpose(0, 3, 1, 2)
            od_s.append((back(kt_new), back(vt_new)))
            x = finish_layer(l, x, [y], wo_odd)
    y_sample = x.reshape(Bd, Q, D)

    stack = lambda states: [jnp.stack(z) for z in zip(*states)]
    k_p, v_p, idx_p, gla_p = stack(ev_p)
    swk_p, swv_p = stack(od_p)
    k_s, v_s, idx_s, gla_s = stack(ev_s)
    swk_s, swv_s = stack(od_s)
    return (y_prompt, y_sample, k_p, v_p, idx_p, gla_p, swk_p, swv_p, k_s, v_s, idx_s, gla_s, swk_s, swv_s)
```
